```python
import jax, jax.numpy as jnp
from jax import lax
import numpy as np

D_MODEL = 1024
BATCH = 8
SEQ = 8192
DEPTH = 2

GRID_W = 64
CTX_LEN = 256
EPS = 1e-6
GLA_HEADS = 4
GLA_DK = 128
GLA_DV = 256
GLA_LOWRANK = 16
GLA_GATE_NORM = 16.0
GLA_CHUNK = 64
FNET_GROUPS = 4
FNET_GROUP_W = 128
CONF_W = 512
CONF_KERNEL = 31
SCONV_W = 512
SCONV_KERNEL = 3
N_EXPERTS = 32
TOP_K = 4
D_FF_EXPERT = D_MODEL
SWIGLU_LIMIT = 7.0
SWIGLU_ALPHA = 1.702
MOE_BLOCK = 512

GLA_KW = GLA_HEADS * GLA_DK
GLA_VW = GLA_HEADS * GLA_DV
FNET_W = FNET_GROUPS * FNET_GROUP_W
EVEN_SPLITS = (GLA_KW, GLA_KW, GLA_VW, GLA_VW, GLA_LOWRANK, GLA_LOWRANK, FNET_W)
EVEN_IN = sum(EVEN_SPLITS)
EVEN_OUT = GLA_VW + FNET_W
ODD_SPLITS = (CONF_W, CONF_W, SCONV_W, SCONV_W, SCONV_W)
ODD_IN = sum(ODD_SPLITS)
ODD_OUT = CONF_W + SCONV_W

kernel_name = 'hybrid_gla_fnet_conformer_shortconv_moe_dit'


def _split(t, widths):
    cuts = [int(v) for v in np.cumsum(widths)[:-1]]
    return jnp.split(t, cuts, axis=-1)


def _heads(t, n):
    b_, l_, w_ = t.shape
    return t.reshape(b_, l_, n, w_ // n).transpose(0, 2, 1, 3)


def rms_norm(x, g):
    xf = x.astype(jnp.float32)
    y = xf * lax.rsqrt(jnp.mean(xf * xf, axis=-1, keepdims=True) + EPS)
    return (y * g.astype(jnp.float32)).astype(x.dtype)


def layer_norm(x, g, b):
    xf = x.astype(jnp.float32)
    mu = jnp.mean(xf, axis=-1, keepdims=True)
    xc = xf - mu
    y = xc * lax.rsqrt(jnp.mean(xc * xc, axis=-1, keepdims=True) + EPS)
    return (y * g.astype(jnp.float32) + b.astype(jnp.float32)).astype(x.dtype)


def adaln(cond, w, b):
    return jnp.split(jax.nn.silu(cond) @ w + b, 6, axis=-1)


def gla_log_decay(lr, up, bias):
    g = jax.nn.log_sigmoid(lr.astype(jnp.float32) @ up.astype(jnp.float32) + bias.astype(jnp.float32))
    return _heads(g / GLA_GATE_NORM, GLA_HEADS)


def gla_chunk_scan(q, k, v, g, s0):
    b_, h_, l_, dk = q.shape
    dv = v.shape[-1]
    n = l_ // GLA_CHUNK

    def to_chunks(t):
        return jnp.moveaxis(t.reshape(b_, h_, n, GLA_CHUNK, t.shape[-1]), 2, 0)

    lower = jnp.tril(jnp.ones((GLA_CHUNK, GLA_CHUNK), bool))[:, :, None]

    def step(s, inp):
        qi, ki, vi, gi = inp
        bcum = jnp.cumsum(gi, axis=-2)
        diff = bcum[..., :, None, :] - bcum[..., None, :, :]
        decay = jnp.exp(jnp.where(lower, diff, -jnp.inf))
        scores = jnp.einsum('bhid,bhjd,bhijd->bhij', qi, ki, decay)
        o = jnp.einsum('bhij,bhjv->bhiv', scores, vi) + jnp.einsum('bhid,bhdv->bhiv', qi * jnp.exp(bcum), s)
        b_last = bcum[..., -1:, :]
        s_new = jnp.exp(b_last)[..., 0, :, None] * s + jnp.einsum('bhjd,bhjv->bhdv', ki * jnp.exp(b_last - bcum), vi)
        return s_new, o

    s_fin, o = lax.scan(step, s0.astype(jnp.float32), tuple(to_chunks(t) for t in (q, k, v, g)))
    o = jnp.moveaxis(o, 0, 2).reshape(b_, h_, l_, dv)
    return o, s_fin


def gla_final_state(k, v, g):
    bcum = jnp.cumsum(g, axis=2)
    w = jnp.exp(bcum[:, :, -1:, :] - bcum)
    return jnp.einsum('bhld,bhlv->bhdv', k * w, v)


def fourier_mix(f):
    b_, l_, _ = f.shape
    fg = f.astype(jnp.float32).reshape(b_, l_, FNET_GROUPS, FNET_GROUP_W).transpose(0, 2, 1, 3)
    out = jnp.fft.fft2(fg, norm='ortho').real
    return out.transpose(0, 2, 1, 3).reshape(b_, l_, FNET_W).astype(f.dtype)


def even_mixer(p, gk_up, gk_b, gnorm, w_out, s_fwd, s_bwd):
    b_, l_, _ = p.shape
    q, k, v, r, lf, lb, f = _split(p, EVEN_SPLITS)
    q = _heads(q, GLA_HEADS).astype(jnp.float32) * (GLA_DK ** -0.5)
    k = _heads(k, GLA_HEADS).astype(jnp.float32)
    v = _heads(v, GLA_HEADS).astype(jnp.float32)
    g_f = gla_log_decay(lf, gk_up[0], gk_b[0])
    g_b = gla_log_decay(lb, gk_up[1], gk_b[1])
    o_f, s_f = gla_chunk_scan(q, k, v, g_f, s_fwd)
    o_b, s_b = gla_chunk_scan(jnp.flip(q, 2), jnp.flip(k, 2), jnp.flip(v, 2), jnp.flip(g_b, 2), s_bwd)
    o = (o_f + jnp.flip(o_b, 2)).transpose(0, 2, 1, 3)
    o = rms_norm(o, gnorm) * jax.nn.silu(r.astype(jnp.float32)).reshape(b_, l_, GLA_HEADS, GLA_DV)
    o = o.reshape(b_, l_, GLA_VW).astype(p.dtype)
    y = jnp.concatenate([o, fourier_mix(f)], axis=-1) @ w_out
    return y, s_f, s_b


def gla_context_states(pc, gk_up, gk_b):
    _, k, v, _, lf, lb, _ = _split(pc, EVEN_SPLITS)
    k = _heads(k, GLA_HEADS).astype(jnp.float32)
    v = _heads(v, GLA_HEADS).astype(jnp.float32)
    g_f = gla_log_decay(lf, gk_up[0], gk_b[0])
    g_b = gla_log_decay(lb, gk_up[1], gk_b[1])
    s_f = gla_final_state(k, v, g_f)
    s_b = gla_final_state(jnp.flip(k, 2), jnp.flip(v, 2), jnp.flip(g_b, 2))
    return s_f, s_b


def depthwise_conv_seq(x, w):
    pad = w.shape[0] // 2
    return lax.conv_general_dilated(x, w[:, None, :].astype(x.dtype), (1,), [(pad, pad)],
                                    dimension_numbers=('NWC', 'WIO', 'NWC'), feature_group_count=x.shape[-1])


def depthwise_conv_columns(x, w, rows):
    b_, l_, ch = x.shape
    pad = w.shape[0] // 2
    xg = x.reshape(b_, rows, GRID_W, ch)
    y = lax.conv_general_dilated(xg, w[:, None, None, :].astype(x.dtype), (1, 1), [(pad, pad), (0, 0)],
                                 dimension_numbers=('NHWC', 'HWIO', 'NHWC'), feature_group_count=ch)
    return y.reshape(b_, l_, ch)


def odd_mixer(p, conf_dw, conf_dw_b, ln_g, ln_b, sconv_w, w_out, rows):
    a, a_gate, gate_b, gate_c, u = _split(p, ODD_SPLITS)
    hc = a * jax.nn.sigmoid(a_gate)
    hc = depthwise_conv_seq(hc, conf_dw) + conf_dw_b.astype(hc.dtype)
    hc = jax.nn.silu(layer_norm(hc, ln_g, ln_b))
    z = gate_c * u
    z = depthwise_conv_seq(z, sconv_w) if rows is None else depthwise_conv_columns(z, sconv_w, rows)
    z = gate_b * z
    return jnp.concatenate([hc, z], axis=-1) @ w_out


def moe_ffn(h, router_w, router_b, w_gu, b_gu, w_down, b_down):
    t_, d_ = h.shape
    logits = h.astype(jnp.float32) @ router_w.astype(jnp.float32) + router_b.astype(jnp.float32)
    top_val, top_idx = lax.top_k(logits, TOP_K)
    gate = jax.nn.softmax(top_val, axis=-1)
    n_slots = t_ * TOP_K
    flat_e = top_idx.reshape(-1)
    order = jnp.argsort(flat_e)
    sorted_e = flat_e[order]
    slot_tok = order // TOP_K
    counts = jnp.zeros((N_EXPERTS,), jnp.int32).at[flat_e].add(1)
    padded = (counts + MOE_BLOCK - 1) // MOE_BLOCK * MOE_BLOCK
    start_sorted = jnp.cumsum(counts) - counts
    pad_end = jnp.cumsum(padded)
    start_pad = pad_end - padded
    dest = start_pad[sorted_e] + jnp.arange(n_slots, dtype=jnp.int32) - start_sorted[sorted_e]
    n_blocks = (n_slots + N_EXPERTS * (MOE_BLOCK - 1) + MOE_BLOCK - 1) // MOE_BLOCK
    buf_tok = jnp.zeros((n_blocks * MOE_BLOCK,), jnp.int32).at[dest].set(slot_tok)
    block_e = jnp.minimum(jnp.searchsorted(pad_end, jnp.arange(n_blocks, dtype=jnp.int32) * MOE_BLOCK, side='right'),
                          N_EXPERTS - 1)
    xb = h[buf_tok].reshape(n_blocks, MOE_BLOCK, d_)

    def expert_block(args):
        xblk, e = args
        gu = xblk @ w_gu[e] + b_gu[e]
        g, u = gu[..., :D_FF_EXPERT], gu[..., D_FF_EXPERT:]
        g = jnp.minimum(g, SWIGLU_LIMIT)
        u = jnp.clip(u, -SWIGLU_LIMIT, SWIGLU_LIMIT)
        act = (u + 1) * (g * jax.nn.sigmoid(SWIGLU_ALPHA * g))
        return act @ w_down[e] + b_down[e]

    yb = lax.map(expert_block, (xb, block_e)).reshape(n_blocks * MOE_BLOCK, d_)
    y_slot = yb[dest] * gate.reshape(-1)[order][:, None].astype(yb.dtype)
    return jax.ops.segment_sum(y_slot, slot_tok, num_segments=t_)


def ffn_sublayer(x, shift, scale, gate, g_pre, g_post, rw, rb, wgu, bgu, wd, bd):
    b_, l_, d_ = x.shape
    h = rms_norm(x, g_pre) * (1 + scale) + shift
    f = moe_ffn(h.reshape(b_ * l_, d_), rw, rb, wgu, bgu, wd, bd).reshape(b_, l_, d_)
    return x + gate * rms_norm(f, g_post)


def setup_inputs(seed: int = 0) -> dict:
    key = jax.random.key(seed)
    ks = iter(jax.random.split(key, 40))

    def nrm(shape, s):
        return s * jax.random.normal(next(ks), shape, jnp.float32)

    n_even = (DEPTH + 1) // 2
    n_odd = DEPTH // 2
    d = D_MODEL
    return {
        'x': nrm((BATCH, SEQ, d), 1.0),
        'c': nrm((BATCH, d), 1.0),
        'ctx': nrm((BATCH, CTX_LEN, d), 1.0),
        'c_ctx': nrm((d,), 1.0),
        'mod_w': nrm((DEPTH, d, 6 * d), 0.5 * d ** -0.5),
        'mod_b': nrm((DEPTH, 6 * d), 0.02),
        'norm_mix_pre': 1.0 + nrm((DEPTH, d), 0.05),
        'norm_mix_post': 1.0 + nrm((DEPTH, d), 0.05),
        'norm_ffn_pre': 1.0 + nrm((DEPTH, d), 0.05),
        'norm_ffn_post': 1.0 + nrm((DEPTH, d), 0.05),
        'ev_w_in': nrm((n_even, d, EVEN_IN), d ** -0.5),
        'ev_gk_up': nrm((n_even, 2, GLA_LOWRANK, GLA_KW), GLA_LOWRANK ** -0.5),
        'ev_gk_b': nrm((n_even, 2, GLA_KW), 0.1),
        'ev_gnorm': 1.0 + nrm((n_even, GLA_DV), 0.05),
        'ev_w_out': nrm((n_even, EVEN_OUT, d), EVEN_OUT ** -0.5),
        'od_w_in': nrm((n_odd, d, ODD_IN), d ** -0.5),
        'od_conf_dw': nrm((n_odd, CONF_KERNEL, CONF_W), CONF_KERNEL ** -0.5),
        'od_conf_dw_b': nrm((n_odd, CONF_W), 0.02),
        'od_conf_ln_g': 1.0 + nrm((n_odd, CONF_W), 0.05),
        'od_conf_ln_b': nrm((n_odd, CONF_W), 0.02),
        'od_sconv': nrm((n_odd, SCONV_KERNEL, SCONV_W), SCONV_KERNEL ** -0.5),
        'od_w_out': nrm((n_odd, ODD_OUT, d), ODD_OUT ** -0.5),
        'router_w': nrm((DEPTH, d, N_EXPERTS), d ** -0.5),
        'router_b': nrm((DEPTH, N_EXPERTS), 0.01),
        'exp_w_gu': nrm((DEPTH, N_EXPERTS, d, 2 * D_FF_EXPERT), d ** -0.5),
        'exp_b_gu': nrm((DEPTH, N_EXPERTS, 2 * D_FF_EXPERT), 0.02),
        'exp_w_down': nrm((DEPTH, N_EXPERTS, D_FF_EXPERT, d), D_FF_EXPERT ** -0.5),
        'exp_b_down': nrm((DEPTH, N_EXPERTS, d), 0.02),
    }


def reference(x, c, ctx, c_ctx, mod_w, mod_b, norm_mix_pre, norm_mix_post, norm_ffn_pre, norm_ffn_post,
              ev_w_in, ev_gk_up, ev_gk_b, ev_gnorm, ev_w_out,
              od_w_in, od_conf_dw, od_conf_dw_b, od_conf_ln_g, od_conf_ln_b, od_sconv, od_w_out,
              router_w, router_b, exp_w_gu, exp_b_gu, exp_w_down, exp_b_down):
    b_, l_, _ = x.shape
    rows = l_ // GRID_W
    for layer in range(DEPTH):
        ctx_carry = any(j % 2 == 0 for j in range(layer + 1, DEPTH))
        sm, cm, gm, sf, cf, gf = [m[:, None, :] for m in adaln(c, mod_w[layer], mod_b[layer])]
        h = rms_norm(x, norm_mix_pre[layer]) * (1 + cm) + sm
        need_ctx = (layer % 2 == 0) or ctx_carry
        if need_ctx:
            c_sm, c_cm, c_gm, c_sf, c_cf, c_gf = adaln(c_ctx, mod_w[layer], mod_b[layer])
            hc = rms_norm(ctx, norm_mix_pre[layer]) * (1 + c_cm) + c_sm
        if layer % 2 == 0:
            e = layer // 2
            pc = hc @ ev_w_in[e]
            if ctx_carry:
                zero = jnp.zeros((b_, GLA_HEADS, GLA_DK, GLA_DV), jnp.float32)
                yc, s_f, s_b = even_mixer(pc, ev_gk_up[e], ev_gk_b[e], ev_gnorm[e], ev_w_out[e], zero, zero)
            else:
                s_f, s_b = gla_context_states(pc, ev_gk_up[e], ev_gk_b[e])
            y, _, _ = even_mixer(h @ ev_w_in[e], ev_gk_up[e], ev_gk_b[e], ev_gnorm[e], ev_w_out[e], s_f, s_b)
        else:
            o = layer // 2
            odd_p = (od_conf_dw[o], od_conf_dw_b[o], od_conf_ln_g[o], od_conf_ln_b[o], od_sconv[o], od_w_out[o])
            y = odd_mixer(h @ od_w_in[o], *odd_p, rows)
            if ctx_carry:
                yc = odd_mixer(hc @ od_w_in[o], *odd_p, None)
        x = x + gm * rms_norm(y, norm_mix_post[layer])
        x = ffn_sublayer(x, sf, cf, gf, norm_ffn_pre[layer], norm_ffn_post[layer], router_w[layer], router_b[layer],
                         exp_w_gu[layer], exp_b_gu[layer], exp_w_down[layer], exp_b_down[layer])
        if ctx_carry:
            ctx = ctx + c_gm * rms_norm(yc, norm_mix_post[layer])
            ctx = ffn_sublayer(ctx, c_sf, c_cf, c_gf, norm_ffn_pre[layer], norm_ffn_post[layer], router_w[layer],
                               router_b[layer], exp_w_gu[layer], exp_b_gu[layer], exp_w_down[layer], exp_b_down[layer])
    return x
```

```python
import functools

import numpy as np
import jax
import jax.numpy as jnp
from jax import lax
from jax.experimental import pallas as pl
from jax.experimental.pallas import tpu as pltpu

F32 = jnp.float32
BF16 = jnp.bfloat16

D_MODEL = 1024
EPS = 1e-6
GRID_W = 64
GLA_HEADS = 4
GLA_DK = 128
GLA_DV = 256
GLA_LOWRANK = 16
GLA_GATE_NORM = 16.0
GLA_KW = GLA_HEADS * GLA_DK
GLA_VW = GLA_HEADS * GLA_DV
FNET_GROUPS = 4
FNET_GROUP_W = 128
FNET_W = FNET_GROUPS * FNET_GROUP_W
CONF_W = 512
CONF_KERNEL = 31
SCONV_W = 512
N_EXPERTS = 32
TOP_K = 4
D_FF = D_MODEL
SWIGLU_LIMIT = 7.0
SWIGLU_ALPHA = 1.702

LANES = 128
SUBLANES = 8
VMEM_LIMIT = 56 * 1024 * 1024

GLA_CHUNK = 128
ROW_TILE = 512


def _cparams(sem):
    return pltpu.CompilerParams(dimension_semantics=sem, vmem_limit_bytes=VMEM_LIMIT)


def _adaln_body(c_ref, w_ref, b_ref, o_ref):
    c = c_ref[...]
    s = c * jax.nn.sigmoid(c)
    o_ref[0] = jnp.dot(s, w_ref[0], precision=lax.Precision.HIGHEST, preferred_element_type=F32) + b_ref[0]


def _adaln(cond, mod_w, mod_b):
    depth, d, n = mod_w.shape
    rows = cond.shape[0]
    tn = 1536
    return pl.pallas_call(
        _adaln_body,
        out_shape=jax.ShapeDtypeStruct((depth, rows, n), F32),
        grid=(depth, n // tn),
        in_specs=[pl.BlockSpec((rows, d), lambda l, j: (0, 0)),
                  pl.BlockSpec((1, d, tn), lambda l, j: (l, 0, j)),
                  pl.BlockSpec((1, 1, tn), lambda l, j: (l, 0, j))],
        out_specs=pl.BlockSpec((1, rows, tn), lambda l, j: (l, 0, j)),
        compiler_params=_cparams(("arbitrary", "arbitrary")),
        name="adaln",
    )(cond, mod_w, mod_b.reshape(depth, 1, n))


def _modnorm(x, g, mod_ref, shift_row, scale_row):
    ms = jnp.mean(x * x, axis=-1, keepdims=True)
    y = x * lax.rsqrt(ms + EPS) * g
    return y * (1.0 + mod_ref[0, scale_row:scale_row + 1, :]) + mod_ref[0, shift_row:shift_row + 1, :]


def _normproj_body(x_ref, mod_ref, g_ref, w_ref, *rest, widths, has_f32_tail):
    if has_f32_tail:
        wt_ref, outs = rest[0], rest[1:]
    else:
        wt_ref, outs = None, rest
    h = _modnorm(x_ref[...], g_ref[...], mod_ref, 0, 1).astype(BF16)
    col = 0
    for o_ref, wd in zip(outs, widths):
        step = 512
        for j in range(0, wd, step):
            o_ref[:, j:j + step] = jnp.dot(h, w_ref[:, col + j:col + j + step],
                                           preferred_element_type=F32).astype(o_ref.dtype)
        col += wd
    if has_f32_tail:
        outs[-1][...] = jnp.dot(h, wt_ref[...], preferred_element_type=F32)


def _norm_proj(x2, mod, rows_per_mod, g, w, widths, w_tail=None):
    t, d = x2.shape
    tm = min(ROW_TILE, t)
    n = w.shape[1]
    assert sum(widths) == n and all(wd % 512 == 0 for wd in widths) and rows_per_mod % tm == 0
    tiles_per_mod = rows_per_mod // tm
    in_specs = [pl.BlockSpec((tm, d), lambda i: (i, 0)),
                pl.BlockSpec((1, 8, d), lambda i: (i // tiles_per_mod, 0, 0)),
                pl.BlockSpec((1, d), lambda i: (0, 0)),
                pl.BlockSpec((d, n), lambda i: (0, 0))]
    args = [x2, mod, g.reshape(1, d), w]
    out_shape = [jax.ShapeDtypeStruct((t, wd), BF16) for wd in widths]
    out_specs = [pl.BlockSpec((tm, wd), lambda i: (i, 0)) for wd in widths]
    if w_tail is not None:
        in_specs.append(pl.BlockSpec((d, LANES), lambda i: (0, 0)))
        args.append(w_tail)
        out_shape.append(jax.ShapeDtypeStruct((t, LANES), F32))
        out_specs.append(pl.BlockSpec((tm, LANES), lambda i: (i, 0)))
    return pl.pallas_call(
        functools.partial(_normproj_body, widths=tuple(widths), has_f32_tail=w_tail is not None),
        out_shape=out_shape,
        grid=(t // tm,),
        in_specs=in_specs,
        out_specs=out_specs,
        compiler_params=_cparams(("parallel",)),
        name="norm_proj",
    )(*args)


def _block_row(x, blk, r):
    c, w = x.shape
    x3 = x.reshape(c // blk, blk, w)
    return jnp.broadcast_to(x3[:, r:r + 1, :], x3.shape).reshape(c, w)


def _gla_direction(q, k, v, a, st, rev):
    c = q.shape[0]
    row = lax.broadcasted_iota(jnp.int32, (c, 1), 0)
    ri = lax.broadcasted_iota(jnp.int32, (c, c), 0)
    ci = lax.broadcasted_iota(jnp.int32, (c, c), 1)
    qk = jnp.sum(q * k, axis=-1, keepdims=True)
    amat = jnp.where(ri == ci, qk, 0.0)
    pq = a
    pk = jnp.ones_like(a)
    half = 1
    while half < c:
        blk = 2 * half
        in_second = (row & half) != 0
        q_side = in_second if not rev else jnp.logical_not(in_second)
        qs = jnp.where(q_side, q * pq, 0.0).astype(BF16)
        ks = jnp.where(q_side, 0.0, k * pk).astype(BF16)
        s = lax.dot_general(qs, ks, (((1,), (1,)), ((), ())), preferred_element_type=F32)
        amat = amat + jnp.where((ri // blk) == (ci // blk), s, 0.0)
        if blk >= SUBLANES:
            if not rev:
                t_near = _block_row(pq, blk, half - 1)
                t_far = _block_row(pq, blk, blk - 1)
            else:
                t_near = _block_row(pq, blk, half)
                t_far = _block_row(pq, blk, 0)
        else:
            t_near = jnp.zeros_like(pq)
            t_far = jnp.zeros_like(pq)
            pos = row & (blk - 1)
            for u in range(blk):
                if (u >= half) != rev:
                    src = (half - 1) if not rev else half
                    t_near = jnp.where(pos == u, pltpu.roll(pq, (u - src) % c, 0), t_near)
                else:
                    src = (blk - 1) if not rev else 0
                    t_far = jnp.where(pos == u, pltpu.roll(pq, (u - src) % c, 0), t_far)
        pq_new = jnp.where(q_side, pq * t_near, pq)
        pk = jnp.where(q_side, pk, pk * t_far)
        pq = pq_new
        half = blk
    o = jnp.dot(amat.astype(BF16), v, preferred_element_type=F32)
    o = o + lax.dot_general((q * pq).astype(BF16), st.astype(BF16), (((1,), (1,)), ((), ())),
                            preferred_element_type=F32)
    total = pq[c - 1:c, :] if not rev else pq[0:1, :]
    kv = lax.dot_general(v, (k * pk).astype(BF16), (((0,), (0,)), ((), ())), preferred_element_type=F32)
    return o, st * total + kv


def _gla_body(qf_ref, kf_ref, vf_ref, lf_ref, qb_ref, kb_ref, vb_ref, lb_ref, up_ref, bias_ref, s0_ref,
              of_ref, ob_ref, sout_ref, st_ref):
    j = pl.program_id(2)

    @pl.when(j == 0)
    def _():
        st_ref[...] = s0_ref[0, :, 0]

    scale = GLA_DK ** -0.5
    for d, (q_ref, k_ref, v_ref, l_ref, o_ref) in enumerate(((qf_ref, kf_ref, vf_ref, lf_ref, of_ref),
                                                             (qb_ref, kb_ref, vb_ref, lb_ref, ob_ref))):
        x = jnp.dot(l_ref[...], up_ref[d], precision=lax.Precision.HIGHEST, preferred_element_type=F32) + bias_ref[d]
        g = (jnp.minimum(x, 0.0) - jnp.log(1.0 + jnp.exp(-jnp.abs(x)))) * (1.0 / GLA_GATE_NORM)
        a = jnp.exp(g)
        q = q_ref[...].astype(F32) * scale
        k = k_ref[...].astype(F32)
        o, st_new = _gla_direction(q, k, v_ref[...], a, st_ref[d], rev=(d == 1))
        o_ref[...] = o.astype(o_ref.dtype)
        st_ref[d] = st_new

    @pl.when(j == pl.num_programs(2) - 1)
    def _():
        sout_ref[0, :, 0] = st_ref[...]


def _gla(q, k, v, lr, up_pad, bias, s0, batch):
    t = q.shape[0]
    seq = t // batch
    c = min(GLA_CHUNK, seq)
    nc = seq // c
    fwd = lambda b, h, j: (b * nc + j, h)
    bwd = lambda b, h, j: (b * nc + nc - 1 - j, h)
    fwd0 = lambda b, h, j: (b * nc + j, 0)
    bwd0 = lambda b, h, j: (b * nc + nc - 1 - j, 0)
    st_spec = pl.BlockSpec((1, 2, 1, GLA_DV, GLA_DK), lambda b, h, j: (b, 0, h, 0, 0))
    return pl.pallas_call(
        _gla_body,
        out_shape=[jax.ShapeDtypeStruct((t, GLA_VW), BF16), jax.ShapeDtypeStruct((t, GLA_VW), BF16),
                   jax.ShapeDtypeStruct(s0.shape, F32)],
        grid=(batch, GLA_HEADS, nc),
        in_specs=[pl.BlockSpec((c, GLA_DK), fwd), pl.BlockSpec((c, GLA_DK), fwd), pl.BlockSpec((c, GLA_DV), fwd),
                  pl.BlockSpec((c, LANES), fwd0),
                  pl.BlockSpec((c, GLA_DK), bwd), pl.BlockSpec((c, GLA_DK), bwd), pl.BlockSpec((c, GLA_DV), bwd),
                  pl.BlockSpec((c, LANES), bwd0),
                  pl.BlockSpec((2, LANES, GLA_DK), lambda b, h, j: (0, 0, h)),
                  pl.BlockSpec((2, 1, GLA_DK), lambda b, h, j: (0, 0, h)),
                  st_spec],
        out_specs=[pl.BlockSpec((c, GLA_DV), fwd), pl.BlockSpec((c, GLA_DV), bwd), st_spec],
        scratch_shapes=[pltpu.VMEM((2, GLA_DV, GLA_DK), F32)],
        compiler_params=_cparams(("parallel", "parallel", "arbitrary")),
        name="gla_scan",
    )(q, k, v, lr, q, k, v, lr, up_pad, bias, s0)


FFT_P = 128


def _fft_tables(seq):
    q = seq // FFT_P
    n_hi = np.arange(q)
    ang1 = 2.0 * np.pi * ((n_hi[:, None] * n_hi[None, :]) % q) / q
    w1 = np.concatenate([np.cos(ang1), -np.sin(ang1)], axis=0)
    n_lo = np.arange(FFT_P)
    k_a = np.arange(FFT_P)
    k_b = np.arange(q)
    k_full = k_b[:, None, None] + q * k_a[None, :, None]
    ang2 = 2.0 * np.pi * ((k_full * n_lo[None, None, :]) % seq) / seq
    c2, s2 = np.cos(ang2), np.sin(ang2)
    w2 = np.concatenate([np.concatenate([c2, s2], axis=2), np.concatenate([-s2, c2], axis=2)], axis=1)
    ch = np.arange(FNET_GROUP_W)
    angc = 2.0 * np.pi * ((ch[:, None] * ch[None, :]) % FNET_GROUP_W) / FNET_GROUP_W
    scale = 1.0 / np.sqrt(float(seq) * FNET_GROUP_W)
    eye = np.eye(FNET_GROUPS)
    wc = np.concatenate([np.kron(eye, np.cos(angc)), np.kron(eye, np.sin(angc))], axis=0) * scale
    return (jnp.asarray(w1, BF16), jnp.asarray(w2, BF16), jnp.asarray(wc, BF16))


def _fft1_body(z_ref, w_ref, o_ref):
    o_ref[0] = jnp.dot(w_ref[...], z_ref[0], preferred_element_type=F32).astype(o_ref.dtype)


def _fft2_body(a_ref, w_ref, wc_ref, o_ref):
    a = a_ref[0, :, 0].reshape(2 * FFT_P, FNET_W)
    z = jnp.dot(w_ref[0], a, preferred_element_type=F32).astype(BF16)
    zz = jnp.concatenate([z[:FFT_P], z[FFT_P:]], axis=-1)
    o_ref[0] = jnp.dot(zz, wc_ref[...], preferred_element_type=F32).astype(o_ref.dtype)


def _fourier_mix(f, batch):
    t = f.shape[0]
    seq = t // batch
    q = seq // FFT_P
    w1, w2, wc = _fft_tables(seq)
    cols = FFT_P * FNET_W
    tn = min(8192, cols)
    z = f.reshape(batch, q, cols)
    a = pl.pallas_call(
        _fft1_body,
        out_shape=jax.ShapeDtypeStruct((batch, 2 * q, cols), BF16),
        grid=(batch, cols // tn),
        in_specs=[pl.BlockSpec((1, q, tn), lambda b, j: (b, 0, j)),
                  pl.BlockSpec((2 * q, q), lambda b, j: (0, 0))],
        out_specs=pl.BlockSpec((1, 2 * q, tn), lambda b, j: (b, 0, j)),
        compiler_params=_cparams(("parallel", "parallel")),
        name="fft_stage1",
    )(z, w1)
    a = a.reshape(batch, 2, q, FFT_P, FNET_W)
    out = pl.pallas_call(
        _fft2_body,
        out_shape=jax.ShapeDtypeStruct((batch, FFT_P, q * FNET_W), BF16),
        grid=(batch, q),
        in_specs=[pl.BlockSpec((1, 2, 1, FFT_P, FNET_W), lambda b, kb: (b, 0, kb, 0, 0)),
                  pl.BlockSpec((1, 2 * FFT_P, 2 * FFT_P), lambda b, kb: (kb, 0, 0)),
                  pl.BlockSpec((2 * FNET_W, FNET_W), lambda b, kb: (0, 0))],
        out_specs=pl.BlockSpec((1, FFT_P, FNET_W), lambda b, kb: (b, 0, kb)),
        compiler_params=_cparams(("parallel", "parallel")),
        name="fft_stage2",
    )(a, w2, wc)
    return out.reshape(batch, FFT_P, q, FNET_W).reshape(t, FNET_W)


def _rms(y, g):
    return y * lax.rsqrt(jnp.mean(y * y, axis=-1, keepdims=True) + EPS) * g


def _even_out_body(of_ref, ob_ref, r_ref, fo_ref, x_ref, mod_ref, gn_ref, w_ref, gpost_ref, o_ref):
    parts = []
    for h in range(GLA_HEADS):
        sl = slice(h * GLA_DV, (h + 1) * GLA_DV)
        o = of_ref[:, sl].astype(F32) + ob_ref[:, sl].astype(F32)
        r = r_ref[:, sl].astype(F32)
        parts.append((_rms(o, gn_ref[...]) * (r * jax.nn.sigmoid(r))).astype(BF16))
    y = jnp.dot(jnp.concatenate(parts, axis=-1), w_ref[:GLA_VW, :], preferred_element_type=F32)
    y = y + jnp.dot(fo_ref[...], w_ref[GLA_VW:, :], preferred_element_type=F32)
    o_ref[...] = x_ref[...] + mod_ref[0, 2:3, :] * _rms(y, gpost_ref[...])


def _even_out(o_f, o_b, r, fo, x2, mod, gnorm, w_out, g_post, batch):
    t, d = x2.shape
    tm = min(ROW_TILE, t)
    tiles_per_mod = (t // batch) // tm
    row = lambda w: pl.BlockSpec((tm, w), lambda i: (i, 0))
    full = lambda a: pl.BlockSpec(a.shape, lambda i: (0,) * a.ndim)
    gn = gnorm.reshape(1, GLA_DV)
    gp = g_post.reshape(1, d)
    return pl.pallas_call(
        _even_out_body,
        out_shape=jax.ShapeDtypeStruct((t, d), F32),
        grid=(t // tm,),
        in_specs=[row(GLA_VW), row(GLA_VW), row(GLA_VW), row(FNET_W), row(d),
                  pl.BlockSpec((1, 8, d), lambda i: (i // tiles_per_mod, 0, 0)),
                  full(gn), full(w_out), full(gp)],
        out_specs=row(d),
        compiler_params=_cparams(("parallel",)),
        name="even_out",
    )(o_f, o_b, r, fo, x2, mod, gn, w_out, gp)


MOE_TILE = 512
MOE_BLOCK = 512
NEG_BIG = -1e30


def _route_body(x_ref, mod_ref, g_ref, rw_ref, rb_ref, h_ref, lpos_ref, gate_ref, cnt_ref, cbase_ref, base_ref):
    i = pl.program_id(0)

    @pl.when(i == 0)
    def _():
        base_ref[...] = jnp.zeros_like(base_ref)

    tm = x_ref.shape[0]
    h = _modnorm(x_ref[...], g_ref[...], mod_ref, 3, 4)
    h_ref[...] = h.astype(BF16)
    logits = jnp.dot(h, rw_ref[...], precision=lax.Precision.HIGHEST, preferred_element_type=F32) + rb_ref[...]
    lane = lax.broadcasted_iota(jnp.int32, (tm, LANES), 1).astype(F32)
    work = logits
    vals, hots = [], []
    for _k in range(TOP_K):
        m = jnp.max(work, axis=-1, keepdims=True)
        idx = jnp.min(jnp.where(work == m, lane, float(LANES)), axis=-1, keepdims=True)
        hot = lane == idx
        vals.append(m)
        hots.append(hot)
        work = jnp.where(hot, -jnp.inf, work)
    es = [jnp.exp(v - vals[0]) for v in vals]
    inv = 1.0 / (es[0] + es[1] + es[2] + es[3])
    sel = sum(hh.astype(F32) for hh in hots)
    ri = lax.broadcasted_iota(jnp.int32, (tm, tm), 0)
    ci = lax.broadcasted_iota(jnp.int32, (tm, tm), 1)
    earlier = jnp.where(ci < ri, 1.0, 0.0).astype(BF16)
    cum = jnp.dot(earlier, sel.astype(BF16), preferred_element_type=F32)
    cnt = jnp.sum(sel, axis=0, keepdims=True)
    ei = lax.broadcasted_iota(jnp.int32, (LANES, LANES), 0)
    ej = lax.broadcasted_iota(jnp.int32, (LANES, LANES), 1)
    before = jnp.where(ei < ej, 1.0, 0.0)
    loff = jnp.dot(jnp.broadcast_to(cnt, (SUBLANES, LANES)), before, precision=lax.Precision.HIGHEST,
                   preferred_element_type=F32)[0:1]
    where_to = cum + loff
    for k in range(TOP_K):
        p = jnp.sum(jnp.where(hots[k], where_to, 0.0), axis=-1, keepdims=True)
        lpos_ref[:, k:k + 1] = p.astype(jnp.int32)
        gate_ref[:, k:k + 1] = es[k] * inv
    cnt_ref[0] = cnt.astype(jnp.int32)
    cbase_ref[0] = base_ref[...].astype(jnp.int32)
    base_ref[...] = base_ref[...] + cnt


def _route(x2, mod, g_pre, rw, rb, batch):
    t, d = x2.shape
    tm = min(MOE_TILE, t)
    nt = t // tm
    tiles_per_mod = (t // batch) // tm
    rw_pad = jnp.zeros((d, LANES), F32).at[:, :N_EXPERTS].set(rw)
    rb_pad = jnp.full((1, LANES), NEG_BIG, F32).at[0, :N_EXPERTS].set(rb)
    return pl.pallas_call(
        _route_body,
        out_shape=[jax.ShapeDtypeStruct((t, d), BF16), jax.ShapeDtypeStruct((t, TOP_K), jnp.int32),
                   jax.ShapeDtypeStruct((t, TOP_K), F32), jax.ShapeDtypeStruct((nt, 1, LANES), jnp.int32),
                   jax.ShapeDtypeStruct((nt, 1, LANES), jnp.int32)],
        grid=(nt,),
        in_specs=[pl.BlockSpec((tm, d), lambda i: (i, 0)),
                  pl.BlockSpec((1, 8, d), lambda i: (i // tiles_per_mod, 0, 0)),
                  pl.BlockSpec((1, d), lambda i: (0, 0)),
                  pl.BlockSpec((d, LANES), lambda i: (0, 0)),
                  pl.BlockSpec((1, LANES), lambda i: (0, 0))],
        out_specs=[pl.BlockSpec((tm, d), lambda i: (i, 0)), pl.BlockSpec((tm, TOP_K), lambda i: (i, 0)),
                   pl.BlockSpec((tm, TOP_K), lambda i: (i, 0)), pl.BlockSpec((1, 1, LANES), lambda i: (i, 0, 0)),
                   pl.BlockSpec((1, 1, LANES), lambda i: (i, 0, 0))],
        scratch_shapes=[pltpu.VMEM((1, LANES), F32)],
        compiler_params=_cparams(("arbitrary",)),
        name="moe_route",
    )(x2, mod, g_pre.reshape(1, d), rw_pad, rb_pad)


def _moe_plan(cnt, cbase, n_slots):
    cnt = cnt[:, 0, :N_EXPERTS]
    cbase = cbase[:, 0, :N_EXPERTS]
    total = cbase[-1] + cnt[-1]
    padded = (total + MOE_BLOCK - 1) // MOE_BLOCK * MOE_BLOCK
    pad_end = jnp.cumsum(padded)
    start = pad_end - padded
    n_blocks = (n_slots + N_EXPERTS * (MOE_BLOCK - 1) + MOE_BLOCK - 1) // MOE_BLOCK
    n_used = pad_end[-1] // MOE_BLOCK
    blk = jnp.arange(n_blocks, dtype=jnp.int32)
    block_e = jnp.minimum(jnp.searchsorted(pad_end, jnp.minimum(blk, n_used - 1) * MOE_BLOCK, side='right'),
                          N_EXPERTS - 1).astype(jnp.int32)
    loff = jnp.cumsum(cnt, axis=1) - cnt
    gdst = start[None, :] + cbase
    flat = lambda a: a.reshape(-1).astype(jnp.int32)
    fill = (flat(padded - total), jnp.zeros((N_EXPERTS,), jnp.int32), flat(start + total))
    return (n_blocks, block_e, n_used.reshape(1).astype(jnp.int32), flat(cnt), flat(loff), flat(gdst)) + fill


def _strip_copies(cnt_s, loff_s, gdst_s, tile, local_ref, global_ref, sem, to_global, max_rows):
    def per_expert(e, carry):
        n = cnt_s[tile * N_EXPERTS + e]
        src0 = loff_s[tile * N_EXPERTS + e]
        dst0 = gdst_s[tile * N_EXPERTS + e]
        off = jnp.int32(0)
        size = max_rows
        while size >= 1:
            take = n & size

            @pl.when(take != 0)
            def _(off=off, size=size):
                loc = local_ref.at[pl.ds(src0 + off, size)]
                glo = global_ref.at[pl.ds(dst0 + off, size)]
                if to_global:
                    pltpu.make_async_copy(loc, glo, sem).start()
                else:
                    pltpu.make_async_copy(glo, loc, sem).start()

            off = off + take
            size //= 2
        return carry

    lax.fori_loop(0, N_EXPERTS, per_expert, 0)


def _dispatch_body(cnt_s, loff_s, gdst_s, zcnt_s, zoff_s, zdst_s, nu_s, h_ref, lpos_ref, xb_ref, xs_ref, zero_ref, sem,
                   zsem, *, n_blocks, n_fill):
    i = pl.program_id(0)
    tm = h_ref.shape[0]
    rows = TOP_K * tm
    lane = lax.broadcasted_iota(jnp.int32, (tm, rows), 1)
    onehot = jnp.zeros((tm, rows), F32)
    for k in range(TOP_K):
        onehot = onehot + jnp.where(lane == lpos_ref[:, k:k + 1], 1.0, 0.0)
    xs = lax.dot_general(onehot.astype(BF16), h_ref[...], (((0,), (0,)), ((), ())), preferred_element_type=F32)
    xs_ref[...] = xs.reshape(rows, 1, xs.shape[-1])
    _strip_copies(cnt_s, loff_s, gdst_s, i, xs_ref, xb_ref, sem, True, tm)

    @pl.when(i == pl.num_programs(0) - 1)
    def _():
        zero_ref[...] = jnp.zeros_like(zero_ref)
        _strip_copies(zcnt_s, zoff_s, zdst_s, 0, zero_ref, xb_ref, zsem, True, MOE_BLOCK // 2)

        def per_block(b, carry):
            @pl.when(b >= nu_s[0])
            def _():
                pltpu.make_async_copy(zero_ref, xb_ref.at[pl.ds(b * MOE_BLOCK, MOE_BLOCK)], zsem).start()
            return carry

        lax.fori_loop(0, n_blocks, per_block, 0)
        pltpu.make_async_copy(xb_ref.at[pl.ds(0, n_fill)], xb_ref.at[pl.ds(0, n_fill)], zsem).wait()

    pltpu.make_async_copy(xs_ref, xb_ref.at[pl.ds(0, rows)], sem).wait()


def _dispatch(h, lpos, plan):
    n_blocks, _, n_used, cnt_s, loff_s, gdst_s, zcnt_s, zoff_s, zdst_s = plan
    t, d = h.shape
    tm = min(MOE_TILE, t)
    rows = TOP_K * tm
    n_fill = n_blocks * MOE_BLOCK - TOP_K * t
    return pl.pallas_call(
        functools.partial(_dispatch_body, n_blocks=n_blocks, n_fill=n_fill),
        out_shape=jax.ShapeDtypeStruct((n_blocks * MOE_BLOCK, 1, d), F32),
        grid_spec=pltpu.PrefetchScalarGridSpec(
            num_scalar_prefetch=7,
            grid=(t // tm,),
            in_specs=[pl.BlockSpec((tm, d), lambda i, *_: (i, 0)),
                      pl.BlockSpec((tm, TOP_K), lambda i, *_: (i, 0))],
            out_specs=pl.BlockSpec(memory_space=pl.ANY),
            scratch_shapes=[pltpu.VMEM((rows, 1, d), F32), pltpu.VMEM((MOE_BLOCK, 1, d), F32),
                            pltpu.SemaphoreType.DMA, pltpu.SemaphoreType.DMA],
        ),
        compiler_params=_cparams(("arbitrary",)),
        name="moe_dispatch",
    )(cnt_s, loff_s, gdst_s, zcnt_s, zoff_s, zdst_s, n_used, h, lpos)


def _experts_body(be_s, nu_s, x_ref, wgu_ref, bgu_ref, wd_ref, bd_ref, o_ref):
    i = pl.program_id(0)
    rows, _, d = x_ref.shape

    @pl.when(i < nu_s[0])
    def _():
        x = x_ref[...].reshape(rows, d).astype(BF16)
        gu = jnp.dot(x, wgu_ref[0], preferred_element_type=F32) + bgu_ref[0]
        g = jnp.minimum(gu[:, :D_FF], SWIGLU_LIMIT)
        u = jnp.clip(gu[:, D_FF:], -SWIGLU_LIMIT, SWIGLU_LIMIT)
        act = (u + 1.0) * (g * jax.nn.sigmoid(SWIGLU_ALPHA * g))
        y = jnp.dot(act.astype(BF16), wd_ref[0], preferred_element_type=F32) + bd_ref[0]
        o_ref[...] = y.reshape(rows, 1, d)

    @pl.when(i >= nu_s[0])
    def _():
        o_ref[...] = jnp.zeros_like(o_ref)


def _experts(xb, plan, w_gu, b_gu, w_down, b_down):
    n_blocks, block_e, n_used = plan[:3]
    d = xb.shape[-1]
    return pl.pallas_call(
        _experts_body,
        out_shape=jax.ShapeDtypeStruct(xb.shape, F32),
        grid_spec=pltpu.PrefetchScalarGridSpec(
            num_scalar_prefetch=2,
            grid=(n_blocks,),
            in_specs=[pl.BlockSpec((MOE_BLOCK, 1, d), lambda i, be, nu: (i, 0, 0)),
                      pl.BlockSpec((1, d, 2 * D_FF), lambda i, be, nu: (be[i], 0, 0)),
                      pl.BlockSpec((1, 1, 2 * D_FF), lambda i, be, nu: (be[i], 0, 0)),
                      pl.BlockSpec((1, D_FF, d), lambda i, be, nu: (be[i], 0, 0)),
                      pl.BlockSpec((1, 1, d), lambda i, be, nu: (be[i], 0, 0))],
            out_specs=pl.BlockSpec((MOE_BLOCK, 1, d), lambda i, be, nu: (i, 0, 0)),
        ),
        compiler_params=_cparams(("arbitrary",)),
        name="moe_experts",
    )(block_e, n_used, xb, w_gu, b_gu.reshape(N_EXPERTS, 1, -1), w_down, b_down.reshape(N_EXPERTS, 1, -1))


def _combine_body(cnt_s, loff_s, gdst_s, yb_ref, lpos_ref, gate_ref, x_ref, mod_ref, g_ref, o_ref, ys_ref, sem):
    i = pl.program_id(0)
    tm = x_ref.shape[0]
    rows = TOP_K * tm
    _strip_copies(cnt_s, loff_s, gdst_s, i, ys_ref, yb_ref, sem, False, tm)
    lane = lax.broadcasted_iota(jnp.int32, (tm, rows), 1)
    weights = jnp.zeros((tm, rows), F32)
    for k in range(TOP_K):
        weights = weights + jnp.where(lane == lpos_ref[:, k:k + 1], gate_ref[:, k:k + 1], 0.0)
    pltpu.make_async_copy(yb_ref.at[pl.ds(0, rows)], ys_ref, sem).wait()
    ys = ys_ref[...].reshape(rows, ys_ref.shape[-1]).astype(BF16)
    f = jnp.dot(weights.astype(BF16), ys, preferred_element_type=F32)
    o_ref[...] = x_ref[...] + mod_ref[0, 5:6, :] * _rms(f, g_ref[...])


def _combine(yb, lpos, gate, x2, mod, g_post, plan, batch):
    cnt_s, loff_s, gdst_s = plan[3:6]
    t, d = x2.shape
    tm = min(MOE_TILE, t)
    rows = TOP_K * tm
    tiles_per_mod = (t // batch) // tm
    return pl.pallas_call(
        _combine_body,
        out_shape=jax.ShapeDtypeStruct((t, d), F32),
        grid_spec=pltpu.PrefetchScalarGridSpec(
            num_scalar_prefetch=3,
            grid=(t // tm,),
            in_specs=[pl.BlockSpec(memory_space=pl.ANY),
                      pl.BlockSpec((tm, TOP_K), lambda i, *_: (i, 0)),
                      pl.BlockSpec((tm, TOP_K), lambda i, *_: (i, 0)),
                      pl.BlockSpec((tm, d), lambda i, *_: (i, 0)),
                      pl.BlockSpec((1, 8, d), lambda i, *_: (i // tiles_per_mod, 0, 0)),
                      pl.BlockSpec((1, d), lambda i, *_: (0, 0))],
            out_specs=pl.BlockSpec((tm, d), lambda i, *_: (i, 0)),
            scratch_shapes=[pltpu.VMEM((rows, 1, d), F32), pltpu.SemaphoreType.DMA],
        ),
        compiler_params=_cparams(("arbitrary",)),
        name="moe_combine",
    )(cnt_s, loff_s, gdst_s, yb, lpos, gate, x2, mod, g_post.reshape(1, d))


def _moe_ffn(x2, mod, g_pre, g_post, rw, rb, w_gu, b_gu, w_down, b_down, batch):
    h, lpos, gate, cnt, cbase = _route(x2, mod, g_pre, rw, rb, batch)
    plan = _moe_plan(cnt, cbase, x2.shape[0] * TOP_K)
    xb = _dispatch(h, lpos, plan)
    yb = _experts(xb, plan, w_gu.astype(BF16), b_gu, w_down.astype(BF16), b_down)
    return _combine(yb, lpos, gate, x2, mod, g_post, plan, batch)


CONF_HALO = 16


def _odd_body(a_ref, ap_ref, an_ref, gb_ref, cu_ref, cup_ref, cun_ref, x_ref, mod_ref, dw_ref, dwb_ref, lng_ref,
              lnb_ref, sc_ref, w_ref, gpost_ref, o_ref, *, tiles_per_seq):
    i = pl.program_id(0)
    tm = x_ref.shape[0]
    first = (i % tiles_per_seq) == 0
    last = (i % tiles_per_seq) == tiles_per_seq - 1

    def glu(ref):
        v = ref[...].astype(F32)
        return v[:, :CONF_W] * jax.nn.sigmoid(v[:, CONF_W:])

    pad = CONF_KERNEL // 2
    ext = jnp.concatenate([jnp.where(first, 0.0, glu(ap_ref)), glu(a_ref), jnp.where(last, 0.0, glu(an_ref))], axis=0)
    acc = jnp.zeros((tm, CONF_W), F32) + dwb_ref[...]
    for k in range(CONF_KERNEL):
        lo = CONF_HALO - pad + k
        acc = acc + dw_ref[k:k + 1, :] * ext[lo:lo + tm, :]
    mu = jnp.mean(acc, axis=-1, keepdims=True)
    xc = acc - mu
    hn = xc * lax.rsqrt(jnp.mean(xc * xc, axis=-1, keepdims=True) + EPS) * lng_ref[...] + lnb_ref[...]
    hc = (hn * jax.nn.sigmoid(hn)).astype(BF16)

    def gated(ref):
        v = ref[...].astype(F32)
        return v[:, :SCONV_W] * v[:, SCONV_W:]

    zc = gated(cu_ref)
    zext = jnp.concatenate([jnp.where(first, 0.0, gated(cup_ref)), zc, jnp.where(last, 0.0, gated(cun_ref))], axis=0)
    z = sc_ref[0:1, :] * zext[0:tm] + sc_ref[1:2, :] * zc + sc_ref[2:3, :] * zext[2 * GRID_W:2 * GRID_W + tm]
    z = (gb_ref[...].astype(F32) * z).astype(BF16)
    y = jnp.dot(hc, w_ref[:CONF_W, :], preferred_element_type=F32) + jnp.dot(z, w_ref[CONF_W:, :],
                                                                            preferred_element_type=F32)
    o_ref[...] = x_ref[...] + mod_ref[0, 2:3, :] * _rms(y, gpost_ref[...])


def _odd_mix(a, gb, cu, x2, mod, conf_dw, conf_dw_b, ln_g, ln_b, sconv, w_out, g_post, batch):
    t, d = x2.shape
    seq = t // batch
    tm = min(ROW_TILE, seq)
    tiles_per_seq = seq // tm
    nt = t // tm
    hb_c, hb_s = tm // CONF_HALO, tm // GRID_W
    row = lambda w: pl.BlockSpec((tm, w), lambda i: (i, 0))
    prev = lambda rows, per, w: pl.BlockSpec((rows, w), lambda i: (jnp.maximum(i * per - 1, 0), 0))
    nxt = lambda rows, per, w: pl.BlockSpec((rows, w), lambda i: (jnp.minimum((i + 1) * per, nt * per - 1), 0))
    full = lambda arr: pl.BlockSpec(arr.shape, lambda i: (0,) * arr.ndim)
    smalls = [conf_dw, conf_dw_b.reshape(1, -1), ln_g.reshape(1, -1), ln_b.reshape(1, -1), sconv, w_out,
              g_post.reshape(1, d)]
    return pl.pallas_call(
        functools.partial(_odd_body, tiles_per_seq=tiles_per_seq),
        out_shape=jax.ShapeDtypeStruct((t, d), F32),
        grid=(nt,),
        in_specs=[row(2 * CONF_W), prev(CONF_HALO, hb_c, 2 * CONF_W), nxt(CONF_HALO, hb_c, 2 * CONF_W),
                  row(SCONV_W),
                  row(2 * SCONV_W), prev(GRID_W, hb_s, 2 * SCONV_W), nxt(GRID_W, hb_s, 2 * SCONV_W),
                  row(d), pl.BlockSpec((1, 8, d), lambda i: (i // tiles_per_seq, 0, 0))] + [full(s) for s in smalls],
        out_specs=row(d),
        compiler_params=_cparams(("parallel",)),
        name="odd_mix",
    )(a, a, a, gb, cu, cu, cu, x2, mod, *smalls)


COND_ROWS = 16


def _mod_rows(mods_layer):
    m = mods_layer.reshape(COND_ROWS, 6, D_MODEL)
    return jnp.concatenate([m, jnp.zeros((COND_ROWS, 2, D_MODEL), F32)], axis=1)


def _even_weights(w_in, gk_up, gk_b):
    cuts = np.cumsum((GLA_KW, GLA_KW, GLA_VW, GLA_VW, GLA_LOWRANK, GLA_LOWRANK, FNET_W))[:-1]
    wq, wk, wv, wr, wlf, wlb, wf = jnp.split(w_in, [int(v) for v in cuts], axis=-1)
    w_main = jnp.concatenate([wq, wk, wv, wr, wf], axis=-1).astype(BF16)
    pad = jnp.zeros((w_in.shape[0], LANES - 2 * GLA_LOWRANK), w_in.dtype)
    w_tail = jnp.concatenate([wlf, wlb, pad], axis=-1).astype(BF16)
    up_pad = jnp.zeros((2, LANES, GLA_KW), F32)
    up_pad = up_pad.at[0, :GLA_LOWRANK].set(gk_up[0]).at[1, GLA_LOWRANK:2 * GLA_LOWRANK].set(gk_up[1])
    return w_main, w_tail, up_pad, gk_b.reshape(2, 1, GLA_KW)


EVEN_WIDTHS = (GLA_KW, GLA_KW, GLA_VW, GLA_VW, FNET_W)


def _even_gla(x2, ctx2, mod, g_pre, w_main, w_tail, up_pad, bias, batch):
    nb = batch
    cq, ck, cv, _, _, clr = _norm_proj(ctx2, mod[nb:nb + 1], ctx2.shape[0], g_pre, w_main, EVEN_WIDTHS, w_tail)
    zero = jnp.zeros((nb, 2, GLA_HEADS, GLA_DV, GLA_DK), F32)
    _, _, s_ctx = _gla(cq, ck, cv, clr, up_pad, bias, zero, nb)
    q, k, v, r, f, lr = _norm_proj(x2, mod[:nb], x2.shape[0] // nb, g_pre, w_main, EVEN_WIDTHS, w_tail)
    o_f, o_b, _ = _gla(q, k, v, lr, up_pad, bias, s_ctx, nb)
    return r, f, o_f, o_b


def kernel(x, c, ctx, c_ctx, mod_w, mod_b, norm_mix_pre, norm_mix_post, norm_ffn_pre, norm_ffn_post, ev_w_in, ev_gk_up, ev_gk_b, ev_gnorm, ev_w_out, od_w_in, od_conf_dw, od_conf_dw_b, od_conf_ln_g, od_conf_ln_b, od_sconv, od_w_out, router_w, router_b, exp_w_gu, exp_b_gu, exp_w_down, exp_b_down):
    nb, seq, d = x.shape
    cond = jnp.concatenate([c, c_ctx[None], jnp.zeros((COND_ROWS - nb - 1, d), F32)], axis=0)
    mods = _adaln(cond, mod_w, mod_b)
    x2 = x.reshape(nb * seq, d)
    ctx2 = ctx.reshape(nb * ctx.shape[1], d)
    depth = mod_w.shape[0]
    assert depth == 2, "layer pattern implemented for one even layer followed by one odd layer"
    mod = _mod_rows(mods[0])
    wm, wt, up, bias = _even_weights(ev_w_in[0], ev_gk_up[0], ev_gk_b[0])
    r, f, o_f, o_b = _even_gla(x2, ctx2, mod, norm_mix_pre[0], wm, wt, up, bias, nb)
    fo = _fourier_mix(f, nb)
    x2 = _even_out(o_f, o_b, r, fo, x2, mod, ev_gnorm[0], ev_w_out[0].astype(BF16), norm_mix_post[0], nb)
    x2 = _moe_ffn(x2, mod, norm_ffn_pre[0], norm_ffn_post[0], router_w[0], router_b[0],
                  exp_w_gu[0], exp_b_gu[0], exp_w_down[0], exp_b_down[0], nb)
    mod = _mod_rows(mods[1])
    a, gb, cu = _norm_proj(x2, mod[:nb], seq, norm_mix_pre[1], od_w_in[0].astype(BF16),
                           (2 * CONF_W, SCONV_W, 2 * SCONV_W))
    x2 = _odd_mix(a, gb, cu, x2, mod, od_conf_dw[0], od_conf_dw_b[0], od_conf_ln_g[0], od_conf_ln_b[0], od_sconv[0],
                  od_w_out[0].astype(BF16), norm_mix_post[1], nb)
    x2 = _moe_ffn(x2, mod, norm_ffn_pre[1], norm_ffn_post[1], router_w[1], router_b[1],
                  exp_w_gu[1], exp_b_gu[1], exp_w_down[1], exp_b_down[1], nb)
    return x2.reshape(nb, seq, d)
```

```python
import functools

import numpy as np
import jax
import jax.numpy as jnp
from jax import lax
from jax.experimental import pallas as pl
from jax.experimental.pallas import tpu as pltpu

F32 = jnp.float32
BF16 = jnp.bfloat16

D_MODEL = 1024
EPS = 1e-6
GRID_W = 64
GLA_HEADS = 4
GLA_DK = 128
GLA_DV = 256
GLA_LOWRANK = 16
GLA_GATE_NORM = 16.0
GLA_KW = GLA_HEADS * GLA_DK
GLA_VW = GLA_HEADS * GLA_DV
FNET_GROUPS = 4
FNET_GROUP_W = 128
FNET_W = FNET_GROUPS * FNET_GROUP_W
CONF_W = 512
CONF_KERNEL = 31
SCONV_W = 512
N_EXPERTS = 32
TOP_K = 4
D_FF = D_MODEL
SWIGLU_LIMIT = 7.0
SWIGLU_ALPHA = 1.702

LANES = 128
SUBLANES = 8
VMEM_LIMIT = 56 * 1024 * 1024

GLA_CHUNK = 128
ROW_TILE = 512


def _cparams(sem):
    return pltpu.CompilerParams(dimension_semantics=sem, vmem_limit_bytes=VMEM_LIMIT)


def _adaln_body(c_ref, w_ref, b_ref, o_ref):
    c = c_ref[...]
    s = c * jax.nn.sigmoid(c)
    o_ref[0] = jnp.dot(s, w_ref[0], precision=lax.Precision.HIGHEST, preferred_element_type=F32) + b_ref[0]


def _adaln(cond, mod_w, mod_b):
    depth, d, n = mod_w.shape
    rows = cond.shape[0]
    tn = 1536
    return pl.pallas_call(
        _adaln_body,
        out_shape=jax.ShapeDtypeStruct((depth, rows, n), F32),
        grid=(depth, n // tn),
        in_specs=[pl.BlockSpec((rows, d), lambda l, j: (0, 0)),
                  pl.BlockSpec((1, d, tn), lambda l, j: (l, 0, j)),
                  pl.BlockSpec((1, 1, tn), lambda l, j: (l, 0, j))],
        out_specs=pl.BlockSpec((1, rows, tn), lambda l, j: (l, 0, j)),
        compiler_params=_cparams(("arbitrary", "arbitrary")),
        name="adaln",
    )(cond, mod_w, mod_b.reshape(depth, 1, n))


def _modnorm(x, g, mod_ref, shift_row, scale_row):
    ms = jnp.mean(x * x, axis=-1, keepdims=True)
    y = x * lax.rsqrt(ms + EPS) * g
    return y * (1.0 + mod_ref[0, scale_row:scale_row + 1, :]) + mod_ref[0, shift_row:shift_row + 1, :]


def _normproj_body(x_ref, mod_ref, g_ref, w_ref, *rest, widths, has_f32_tail):
    if has_f32_tail:
        wt_ref, outs = rest[0], rest[1:]
    else:
        wt_ref, outs = None, rest
    h = _modnorm(x_ref[...], g_ref[...], mod_ref, 0, 1).astype(BF16)
    col = 0
    for o_ref, wd in zip(outs, widths):
        step = 512
        for j in range(0, wd, step):
            o_ref[:, j:j + step] = jnp.dot(h, w_ref[:, col + j:col + j + step],
                                           preferred_element_type=F32).astype(o_ref.dtype)
        col += wd
    if has_f32_tail:
        outs[-1][...] = jnp.dot(h, wt_ref[...], preferred_element_type=F32)


def _norm_proj(x2, mod, rows_per_mod, g, w, widths, w_tail=None):
    t, d = x2.shape
    tm = min(ROW_TILE, t)
    n = w.shape[1]
    assert sum(widths) == n and all(wd % 512 == 0 for wd in widths) and rows_per_mod % tm == 0
    tiles_per_mod = rows_per_mod // tm
    in_specs = [pl.BlockSpec((tm, d), lambda i: (i, 0)),
                pl.BlockSpec((1, 8, d), lambda i: (i // tiles_per_mod, 0, 0)),
                pl.BlockSpec((1, d), lambda i: (0, 0)),
                pl.BlockSpec((d, n), lambda i: (0, 0))]
    args = [x2, mod, g.reshape(1, d), w]
    out_shape = [jax.ShapeDtypeStruct((t, wd), BF16) for wd in widths]
    out_specs = [pl.BlockSpec((tm, wd), lambda i: (i, 0)) for wd in widths]
    if w_tail is not None:
        in_specs.append(pl.BlockSpec((d, LANES), lambda i: (0, 0)))
        args.append(w_tail)
        out_shape.append(jax.ShapeDtypeStruct((t, LANES), F32))
        out_specs.append(pl.BlockSpec((tm, LANES), lambda i: (i, 0)))
    return pl.pallas_call(
        functools.partial(_normproj_body, widths=tuple(widths), has_f32_tail=w_tail is not None),
        out_shape=out_shape,
        grid=(t // tm,),
        in_specs=in_specs,
        out_specs=out_specs,
        compiler_params=_cparams(("parallel",)),
        name="norm_proj",
    )(*args)


def _block_row(x, blk, r):
    c, w = x.shape
    x3 = x.reshape(c // blk, blk, w)
    return jnp.broadcast_to(x3[:, r:r + 1, :], x3.shape).reshape(c, w)


def _gla_direction(q, k, v, a, st, rev):
    c = q.shape[0]
    row = lax.broadcasted_iota(jnp.int32, (c, 1), 0)
    ri = lax.broadcasted_iota(jnp.int32, (c, c), 0)
    ci = lax.broadcasted_iota(jnp.int32, (c, c), 1)
    qk = jnp.sum(q * k, axis=-1, keepdims=True)
    amat = jnp.where(ri == ci, qk, 0.0)
    pq = a
    pk = jnp.ones_like(a)
    half = 1
    while half < c:
        blk = 2 * half
        in_second = (row & half) != 0
        q_side = in_second if not rev else jnp.logical_not(in_second)
        qs = jnp.where(q_side, q * pq, 0.0).astype(BF16)
        ks = jnp.where(q_side, 0.0, k * pk).astype(BF16)
        s = lax.dot_general(qs, ks, (((1,), (1,)), ((), ())), preferred_element_type=F32)
        amat = amat + jnp.where((ri // blk) == (ci // blk), s, 0.0)
        if blk >= SUBLANES:
            if not rev:
                t_near = _block_row(pq, blk, half - 1)
                t_far = _block_row(pq, blk, blk - 1)
            else:
                t_near = _block_row(pq, blk, half)
                t_far = _block_row(pq, blk, 0)
        else:
            t_near = jnp.zeros_like(pq)
            t_far = jnp.zeros_like(pq)
            pos = row & (blk - 1)
            for u in range(blk):
                if (u >= half) != rev:
                    src = (half - 1) if not rev else half
                    t_near = jnp.where(pos == u, pltpu.roll(pq, (u - src) % c, 0), t_near)
                else:
                    src = (blk - 1) if not rev else 0
                    t_far = jnp.where(pos == u, pltpu.roll(pq, (u - src) % c, 0), t_far)
        pq_new = jnp.where(q_side, pq * t_near, pq)
        pk = jnp.where(q_side, pk, pk * t_far)
        pq = pq_new
        half = blk
    o = jnp.dot(amat.astype(BF16), v, preferred_element_type=F32)
    o = o + lax.dot_general((q * pq).astype(BF16), st.astype(BF16), (((1,), (1,)), ((), ())),
                            preferred_element_type=F32)
    total = pq[c - 1:c, :] if not rev else pq[0:1, :]
    kv = lax.dot_general(v, (k * pk).astype(BF16), (((0,), (0,)), ((), ())), preferred_element_type=F32)
    return o, st * total + kv


def _gla_body(qf_ref, kf_ref, vf_ref, lf_ref, qb_ref, kb_ref, vb_ref, lb_ref, up_ref, bias_ref, s0_ref,
              of_ref, ob_ref, sout_ref, st_ref):
    j = pl.program_id(2)

    @pl.when(j == 0)
    def _():
        st_ref[...] = s0_ref[0, :, 0]

    scale = GLA_DK ** -0.5
    for d, (q_ref, k_ref, v_ref, l_ref, o_ref) in enumerate(((qf_ref, kf_ref, vf_ref, lf_ref, of_ref),
                                                             (qb_ref, kb_ref, vb_ref, lb_ref, ob_ref))):
        x = jnp.dot(l_ref[...], up_ref[d], precision=lax.Precision.HIGHEST, preferred_element_type=F32) + bias_ref[d]
        g = (jnp.minimum(x, 0.0) - jnp.log(1.0 + jnp.exp(-jnp.abs(x)))) * (1.0 / GLA_GATE_NORM)
        a = jnp.exp(g)
        q = q_ref[...].astype(F32) * scale
        k = k_ref[...].astype(F32)
        o, st_new = _gla_direction(q, k, v_ref[...], a, st_ref[d], rev=(d == 1))
        o_ref[...] = o.astype(o_ref.dtype)
        st_ref[d] = st_new

    @pl.when(j == pl.num_programs(2) - 1)
    def _():
        sout_ref[0, :, 0] = st_ref[...]


def _gla(q, k, v, lr, up_pad, bias, s0, batch):
    t = q.shape[0]
    seq = t // batch
    c = min(GLA_CHUNK, seq)
    nc = seq // c
    fwd = lambda b, h, j: (b * nc + j, h)
    bwd = lambda b, h, j: (b * nc + nc - 1 - j, h)
    fwd0 = lambda b, h, j: (b * nc + j, 0)
    bwd0 = lambda b, h, j: (b * nc + nc - 1 - j, 0)
    st_spec = pl.BlockSpec((1, 2, 1, GLA_DV, GLA_DK), lambda b, h, j: (b, 0, h, 0, 0))
    return pl.pallas_call(
        _gla_body,
        out_shape=[jax.ShapeDtypeStruct((t, GLA_VW), BF16), jax.ShapeDtypeStruct((t, GLA_VW), BF16),
                   jax.ShapeDtypeStruct(s0.shape, F32)],
        grid=(batch, GLA_HEADS, nc),
        in_specs=[pl.BlockSpec((c, GLA_DK), fwd), pl.BlockSpec((c, GLA_DK), fwd), pl.BlockSpec((c, GLA_DV), fwd),
                  pl.BlockSpec((c, LANES), fwd0),
                  pl.BlockSpec((c, GLA_DK), bwd), pl.BlockSpec((c, GLA_DK), bwd), pl.BlockSpec((c, GLA_DV), bwd),
                  pl.BlockSpec((c, LANES), bwd0),
                  pl.BlockSpec((2, LANES, GLA_DK), lambda b, h, j: (0, 0, h)),
                  pl.BlockSpec((2, 1, GLA_DK), lambda b, h, j: (0, 0, h)),
                  st_spec],
        out_specs=[pl.BlockSpec((c, GLA_DV), fwd), pl.BlockSpec((c, GLA_DV), bwd), st_spec],
        scratch_shapes=[pltpu.VMEM((2, GLA_DV, GLA_DK), F32)],
        compiler_params=_cparams(("parallel", "parallel", "arbitrary")),
        name="gla_scan",
    )(q, k, v, lr, q, k, v, lr, up_pad, bias, s0)


FFT_P = 128


def _fft_tables(seq):
    q = seq // FFT_P
    n_hi = np.arange(q)
    ang1 = 2.0 * np.pi * ((n_hi[:, None] * n_hi[None, :]) % q) / q
    w1 = np.concatenate([np.cos(ang1), -np.sin(ang1)], axis=0)
    n_lo = np.arange(FFT_P)
    k_a = np.arange(FFT_P)
    k_b = np.arange(q)
    k_full = k_b[:, None, None] + q * k_a[None, :, None]
    ang2 = 2.0 * np.pi * ((k_full * n_lo[None, None, :]) % seq) / seq
    c2, s2 = np.cos(ang2), np.sin(ang2)
    w2 = np.concatenate([np.concatenate([c2, s2], axis=2), np.concatenate([-s2, c2], axis=2)], axis=1)
    ch = np.arange(FNET_GROUP_W)
    angc = 2.0 * np.pi * ((ch[:, None] * ch[None, :]) % FNET_GROUP_W) / FNET_GROUP_W
    scale = 1.0 / np.sqrt(float(seq) * FNET_GROUP_W)
    eye = np.eye(FNET_GROUPS)
    wc = np.concatenate([np.kron(eye, np.cos(angc)), np.kron(eye, np.sin(angc))], axis=0) * scale
    return (jnp.asarray(w1, BF16), jnp.asarray(w2, BF16), jnp.asarray(wc, BF16))


def _fft1_body(z_ref, w_ref, o_ref):
    o_ref[0] = jnp.dot(w_ref[...], z_ref[0], preferred_element_type=F32).astype(o_ref.dtype)


def _fft2_body(a_ref, w_ref, wc_ref, o_ref):
    a = a_ref[0, :, 0].reshape(2 * FFT_P, FNET_W)
    z = jnp.dot(w_ref[0], a, preferred_element_type=F32).astype(BF16)
    zz = jnp.concatenate([z[:FFT_P], z[FFT_P:]], axis=-1)
    o_ref[0] = jnp.dot(zz, wc_ref[...], preferred_element_type=F32).astype(o_ref.dtype)


def _fourier_mix(f, batch):
    t = f.shape[0]
    seq = t // batch
    q = seq // FFT_P
    w1, w2, wc = _fft_tables(seq)
    cols = FFT_P * FNET_W
    tn = min(8192, cols)
    z = f.reshape(batch, q, cols)
    a = pl.pallas_call(
        _fft1_body,
        out_shape=jax.ShapeDtypeStruct((batch, 2 * q, cols), BF16),
        grid=(batch, cols // tn),
        in_specs=[pl.BlockSpec((1, q, tn), lambda b, j: (b, 0, j)),
                  pl.BlockSpec((2 * q, q), lambda b, j: (0, 0))],
        out_specs=pl.BlockSpec((1, 2 * q, tn), lambda b, j: (b, 0, j)),
        compiler_params=_cparams(("parallel", "parallel")),
        name="fft_stage1",
    )(z, w1)
    a = a.reshape(batch, 2, q, FFT_P, FNET_W)
    out = pl.pallas_call(
        _fft2_body,
        out_shape=jax.ShapeDtypeStruct((batch, FFT_P, q * FNET_W), BF16),
        grid=(batch, q),
        in_specs=[pl.BlockSpec((1, 2, 1, FFT_P, FNET_W), lambda b, kb: (b, 0, kb, 0, 0)),
                  pl.BlockSpec((1, 2 * FFT_P, 2 * FFT_P), lambda b, kb: (kb, 0, 0)),
                  pl.BlockSpec((2 * FNET_W, FNET_W), lambda b, kb: (0, 0))],
        out_specs=pl.BlockSpec((1, FFT_P, FNET_W), lambda b, kb: (b, 0, kb)),
        compiler_params=_cparams(("parallel", "parallel")),
        name="fft_stage2",
    )(a, w2, wc)
    return out.reshape(batch, FFT_P, q, FNET_W).reshape(t, FNET_W)


def _rms(y, g):
    return y * lax.rsqrt(jnp.mean(y * y, axis=-1, keepdims=True) + EPS) * g


def _even_out_body(of_ref, ob_ref, r_ref, fo_ref, x_ref, mod_ref, gn_ref, w_ref, gpost_ref, o_ref):
    parts = []
    for h in range(GLA_HEADS):
        sl = slice(h * GLA_DV, (h + 1) * GLA_DV)
        o = of_ref[:, sl].astype(F32) + ob_ref[:, sl].astype(F32)
        r = r_ref[:, sl].astype(F32)
        parts.append((_rms(o, gn_ref[...]) * (r * jax.nn.sigmoid(r))).astype(BF16))
    y = jnp.dot(jnp.concatenate(parts, axis=-1), w_ref[:GLA_VW, :], preferred_element_type=F32)
    y = y + jnp.dot(fo_ref[...], w_ref[GLA_VW:, :], preferred_element_type=F32)
    o_ref[...] = x_ref[...] + mod_ref[0, 2:3, :] * _rms(y, gpost_ref[...])


def _even_out(o_f, o_b, r, fo, x2, mod, gnorm, w_out, g_post, batch):
    t, d = x2.shape
    tm = min(ROW_TILE, t)
    tiles_per_mod = (t // batch) // tm
    row = lambda w: pl.BlockSpec((tm, w), lambda i: (i, 0))
    full = lambda a: pl.BlockSpec(a.shape, lambda i: (0,) * a.ndim)
    gn = gnorm.reshape(1, GLA_DV)
    gp = g_post.reshape(1, d)
    return pl.pallas_call(
        _even_out_body,
        out_shape=jax.ShapeDtypeStruct((t, d), F32),
        grid=(t // tm,),
        in_specs=[row(GLA_VW), row(GLA_VW), row(GLA_VW), row(FNET_W), row(d),
                  pl.BlockSpec((1, 8, d), lambda i: (i // tiles_per_mod, 0, 0)),
                  full(gn), full(w_out), full(gp)],
        out_specs=row(d),
        compiler_params=_cparams(("parallel",)),
        name="even_out",
    )(o_f, o_b, r, fo, x2, mod, gn, w_out, gp)


MOE_TILE = 512
MOE_BLOCK = 512
NEG_BIG = -1e30


def _route_body(x_ref, mod_ref, g_ref, rw_ref, rb_ref, h_ref, lpos_ref, gate_ref, cnt_ref, cbase_ref, base_ref):
    i = pl.program_id(0)

    @pl.when(i == 0)
    def _():
        base_ref[...] = jnp.zeros_like(base_ref)

    tm = x_ref.shape[0]
    h = _modnorm(x_ref[...], g_ref[...], mod_ref, 3, 4)
    h_ref[...] = h.astype(BF16)
    logits = jnp.dot(h, rw_ref[...], precision=lax.Precision.HIGHEST, preferred_element_type=F32) + rb_ref[...]
    lane = lax.broadcasted_iota(jnp.int32, (tm, LANES), 1).astype(F32)
    work = logits
    vals, hots = [], []
    for _k in range(TOP_K):
        m = jnp.max(work, axis=-1, keepdims=True)
        idx = jnp.min(jnp.where(work == m, lane, float(LANES)), axis=-1, keepdims=True)
        hot = lane == idx
        vals.append(m)
        hots.append(hot)
        work = jnp.where(hot, -jnp.inf, work)
    es = [jnp.exp(v - vals[0]) for v in vals]
    inv = 1.0 / (es[0] + es[1] + es[2] + es[3])
    sel = sum(hh.astype(F32) for hh in hots)
    ri = lax.broadcasted_iota(jnp.int32, (tm, tm), 0)
    ci = lax.broadcasted_iota(jnp.int32, (tm, tm), 1)
    earlier = jnp.where(ci < ri, 1.0, 0.0).astype(BF16)
    cum = jnp.dot(earlier, sel.astype(BF16), preferred_element_type=F32)
    cnt = jnp.sum(sel, axis=0, keepdims=True)
    ei = lax.broadcasted_iota(jnp.int32, (LANES, LANES), 0)
    ej = lax.broadcasted_iota(jnp.int32, (LANES, LANES), 1)
    before = jnp.where(ei < ej, 1.0, 0.0)
    loff = jnp.dot(jnp.broadcast_to(cnt, (SUBLANES, LANES)), before, precision=lax.Precision.HIGHEST,
                   preferred_element_type=F32)[0:1]
    where_to = cum + loff
    for k in range(TOP_K):
        p = jnp.sum(jnp.where(hots[k], where_to, 0.0), axis=-1, keepdims=True)
        lpos_ref[:, k:k + 1] = p.astype(jnp.int32)
        gate_ref[:, k:k + 1] = es[k] * inv
    cnt_ref[0] = cnt.astype(jnp.int32)
    cbase_ref[0] = base_ref[...].astype(jnp.int32)
    base_ref[...] = base_ref[...] + cnt


def _route(x2, mod, g_pre, rw, rb, batch):
    t, d = x2.shape
    tm = min(MOE_TILE, t)
    nt = t // tm
    tiles_per_mod = (t // batch) // tm
    rw_pad = jnp.zeros((d, LANES), F32).at[:, :N_EXPERTS].set(rw)
    rb_pad = jnp.full((1, LANES), NEG_BIG, F32).at[0, :N_EXPERTS].set(rb)
    return pl.pallas_call(
        _route_body,
        out_shape=[jax.ShapeDtypeStruct((t, d), BF16), jax.ShapeDtypeStruct((t, TOP_K), jnp.int32),
                   jax.ShapeDtypeStruct((t, TOP_K), F32), jax.ShapeDtypeStruct((nt, 1, LANES), jnp.int32),
                   jax.ShapeDtypeStruct((nt, 1, LANES), jnp.int32)],
        grid=(nt,),
        in_specs=[pl.BlockSpec((tm, d), lambda i: (i, 0)),
                  pl.BlockSpec((1, 8, d), lambda i: (i // tiles_per_mod, 0, 0)),
                  pl.BlockSpec((1, d), lambda i: (0, 0)),
                  pl.BlockSpec((d, LANES), lambda i: (0, 0)),
                  pl.BlockSpec((1, LANES), lambda i: (0, 0))],
        out_specs=[pl.BlockSpec((tm, d), lambda i: (i, 0)), pl.BlockSpec((tm, TOP_K), lambda i: (i, 0)),
                   pl.BlockSpec((tm, TOP_K), lambda i: (i, 0)), pl.BlockSpec((1, 1, LANES), lambda i: (i, 0, 0)),
                   pl.BlockSpec((1, 1, LANES), lambda i: (i, 0, 0))],
        scratch_shapes=[pltpu.VMEM((1, LANES), F32)],
        compiler_params=_cparams(("arbitrary",)),
        name="moe_route",
    )(x2, mod, g_pre.reshape(1, d), rw_pad, rb_pad)


def _moe_plan(cnt, cbase, n_slots):
    cnt = cnt[:, 0, :N_EXPERTS]
    cbase = cbase[:, 0, :N_EXPERTS]
    total = cbase[-1] + cnt[-1]
    padded = (total + MOE_BLOCK - 1) // MOE_BLOCK * MOE_BLOCK
    pad_end = jnp.cumsum(padded)
    start = pad_end - padded
    n_blocks = (n_slots + N_EXPERTS * (MOE_BLOCK - 1) + MOE_BLOCK - 1) // MOE_BLOCK
    n_used = pad_end[-1] // MOE_BLOCK
    blk = jnp.arange(n_blocks, dtype=jnp.int32)
    first_row = jnp.minimum(blk, n_used - 1) * MOE_BLOCK
    block_e = jnp.minimum(jnp.sum((pad_end[None, :] <= first_row[:, None]).astype(jnp.int32), axis=1), N_EXPERTS - 1)
    loff = jnp.cumsum(cnt, axis=1) - cnt
    gdst = start[None, :] + cbase
    flat = lambda a: a.reshape(-1).astype(jnp.int32)
    fill = (flat(padded - total), jnp.zeros((N_EXPERTS,), jnp.int32), flat(start + total))
    return (n_blocks, block_e, n_used.reshape(1).astype(jnp.int32), flat(cnt), flat(loff), flat(gdst)) + fill


def _strip_copies(cnt_s, loff_s, gdst_s, tile, local_ref, global_ref, sem, to_global, max_rows):
    def per_expert(e, carry):
        n = cnt_s[tile * N_EXPERTS + e]
        src0 = loff_s[tile * N_EXPERTS + e]
        dst0 = gdst_s[tile * N_EXPERTS + e]
        off = jnp.int32(0)
        size = max_rows
        while size >= 1:
            take = n & size

            @pl.when(take != 0)
            def _(off=off, size=size):
                loc = local_ref.at[pl.ds(src0 + off, size)]
                glo = global_ref.at[pl.ds(dst0 + off, size)]
                if to_global:
                    pltpu.make_async_copy(loc, glo, sem).start()
                else:
                    pltpu.make_async_copy(glo, loc, sem).start()

            off = off + take
            size //= 2
        return carry

    lax.fori_loop(0, N_EXPERTS, per_expert, 0)


def _dispatch_body(cnt_s, loff_s, gdst_s, zcnt_s, zoff_s, zdst_s, nu_s, h_ref, lpos_ref, xb_ref, xs0_ref, xs1_ref, zero_ref,
                   sem0, sem1, zsem, *, n_blocks, n_fill, n_tiles):
    i = pl.program_id(0)
    tm = h_ref.shape[0]
    rows = TOP_K * tm
    lane = lax.broadcasted_iota(jnp.int32, (tm, rows), 1)
    onehot = jnp.zeros((tm, rows), F32)
    for k in range(TOP_K):
        onehot = onehot + jnp.where(lane == lpos_ref[:, k:k + 1], 1.0, 0.0)
    xs = lax.dot_general(onehot.astype(BF16), h_ref[...], (((0,), (0,)), ((), ())), preferred_element_type=F32)
    slots = ((xs0_ref, sem0), (xs1_ref, sem1))

    def drain(slot):
        xs_ref, sem = slots[slot]
        pltpu.make_async_copy(xs_ref, xb_ref.at[pl.ds(0, rows)], sem).wait()

    for slot in range(2):
        @pl.when(i % 2 == slot)
        def _(slot=slot):
            @pl.when(i >= 2)
            def _():
                drain(slot)

            xs_ref, sem = slots[slot]
            xs_ref[...] = xs.reshape(rows, 1, xs.shape[-1])
            _strip_copies(cnt_s, loff_s, gdst_s, i, xs_ref, xb_ref, sem, True, tm)

    @pl.when(i == n_tiles - 1)
    def _():
        drain((n_tiles - 1) % 2)
        if n_tiles >= 2:
            drain(n_tiles % 2)
        zero_ref[...] = jnp.zeros_like(zero_ref)
        _strip_copies(zcnt_s, zoff_s, zdst_s, 0, zero_ref, xb_ref, zsem, True, MOE_BLOCK // 2)

        def per_block(b, carry):
            @pl.when(b >= nu_s[0])
            def _():
                pltpu.make_async_copy(zero_ref, xb_ref.at[pl.ds(b * MOE_BLOCK, MOE_BLOCK)], zsem).start()
            return carry

        lax.fori_loop(0, n_blocks, per_block, 0)
        pltpu.make_async_copy(xb_ref.at[pl.ds(0, n_fill)], xb_ref.at[pl.ds(0, n_fill)], zsem).wait()


def _dispatch(h, lpos, plan):
    n_blocks, _, n_used, cnt_s, loff_s, gdst_s, zcnt_s, zoff_s, zdst_s = plan
    t, d = h.shape
    tm = min(MOE_TILE, t)
    rows = TOP_K * tm
    n_fill = n_blocks * MOE_BLOCK - TOP_K * t
    return pl.pallas_call(
        functools.partial(_dispatch_body, n_blocks=n_blocks, n_fill=n_fill, n_tiles=t // tm),
        out_shape=jax.ShapeDtypeStruct((n_blocks * MOE_BLOCK, 1, d), F32),
        grid_spec=pltpu.PrefetchScalarGridSpec(
            num_scalar_prefetch=7,
            grid=(t // tm,),
            in_specs=[pl.BlockSpec((tm, d), lambda i, *_: (i, 0)),
                      pl.BlockSpec((tm, TOP_K), lambda i, *_: (i, 0))],
            out_specs=pl.BlockSpec(memory_space=pl.ANY),
            scratch_shapes=[pltpu.VMEM((rows, 1, d), F32), pltpu.VMEM((rows, 1, d), F32),
                            pltpu.VMEM((MOE_BLOCK, 1, d), F32),
                            pltpu.SemaphoreType.DMA, pltpu.SemaphoreType.DMA, pltpu.SemaphoreType.DMA],
        ),
        compiler_params=_cparams(("arbitrary",)),
        name="moe_dispatch",
    )(cnt_s, loff_s, gdst_s, zcnt_s, zoff_s, zdst_s, n_used, h, lpos)


def _experts_body(be_s, nu_s, x_ref, wgu_ref, bgu_ref, wd_ref, bd_ref, o_ref, x2_ref, wgu_bf, wd_bf):
    i = pl.program_id(0)
    rows, _, d = x_ref.shape

    @pl.when(i < nu_s[0])
    def _():
        @pl.when((i == 0) | (be_s[i] != be_s[jnp.maximum(i - 1, 0)]))
        def _():
            wgu_bf[...] = wgu_ref[0].astype(BF16)
            wd_bf[...] = wd_ref[0].astype(BF16)

        x2_ref[...] = x_ref[...].reshape(rows, d)
        x = x2_ref[...].astype(BF16)
        gu = jnp.dot(x, wgu_bf[...], preferred_element_type=F32) + bgu_ref[0]
        g = jnp.minimum(gu[:, :D_FF], SWIGLU_LIMIT)
        u = jnp.clip(gu[:, D_FF:], -SWIGLU_LIMIT, SWIGLU_LIMIT)
        act = (u + 1.0) * (g * jax.nn.sigmoid(SWIGLU_ALPHA * g))
        y = jnp.dot(act.astype(BF16), wd_bf[...], preferred_element_type=F32) + bd_ref[0]
        o_ref[...] = y.reshape(rows, 1, d)

    @pl.when(i >= nu_s[0])
    def _():
        o_ref[...] = jnp.zeros_like(o_ref)


def _experts(xb, plan, w_gu, b_gu, w_down, b_down):
    n_blocks, block_e, n_used = plan[:3]
    d = xb.shape[-1]
    return pl.pallas_call(
        _experts_body,
        out_shape=jax.ShapeDtypeStruct(xb.shape, F32),
        grid_spec=pltpu.PrefetchScalarGridSpec(
            num_scalar_prefetch=2,
            grid=(n_blocks,),
            in_specs=[pl.BlockSpec((MOE_BLOCK, 1, d), lambda i, be, nu: (i, 0, 0)),
                      pl.BlockSpec((1, d, 2 * D_FF), lambda i, be, nu: (be[i], 0, 0)),
                      pl.BlockSpec((1, 1, 2 * D_FF), lambda i, be, nu: (be[i], 0, 0)),
                      pl.BlockSpec((1, D_FF, d), lambda i, be, nu: (be[i], 0, 0)),
                      pl.BlockSpec((1, 1, d), lambda i, be, nu: (be[i], 0, 0))],
            out_specs=pl.BlockSpec((MOE_BLOCK, 1, d), lambda i, be, nu: (i, 0, 0)),
            scratch_shapes=[pltpu.VMEM((MOE_BLOCK, d), F32), pltpu.VMEM((d, 2 * D_FF), BF16),
                            pltpu.VMEM((D_FF, d), BF16)],
        ),
        compiler_params=_cparams(("arbitrary",)),
        name="moe_experts",
    )(block_e, n_used, xb, w_gu, b_gu.reshape(N_EXPERTS, 1, -1), w_down, b_down.reshape(N_EXPERTS, 1, -1))


def _combine_body(cnt_s, loff_s, gdst_s, yb_ref, lpos_ref, gate_ref, x_ref, mod_ref, g_ref, o_ref, ys0_ref, ys1_ref,
                  ys2_ref, sem0, sem1, *, n_tiles):
    i = pl.program_id(0)
    tm = x_ref.shape[0]
    rows = TOP_K * tm
    slots = ((ys0_ref, sem0), (ys1_ref, sem1))

    def fetch(tile, slot):
        ys_ref, sem = slots[slot]
        _strip_copies(cnt_s, loff_s, gdst_s, tile, ys_ref, yb_ref, sem, False, tm)

    @pl.when(i == 0)
    def _():
        fetch(0, 0)

    for slot in range(2):
        @pl.when(i % 2 == slot)
        def _(slot=slot):
            @pl.when(i + 1 < n_tiles)
            def _():
                fetch(i + 1, 1 - slot)

            ys_ref, sem = slots[slot]
            pltpu.make_async_copy(yb_ref.at[pl.ds(0, rows)], ys_ref, sem).wait()
            ys2_ref[...] = ys_ref[...].reshape(rows, ys_ref.shape[-1])

    lane = lax.broadcasted_iota(jnp.int32, (tm, rows), 1)
    weights = jnp.zeros((tm, rows), F32)
    for k in range(TOP_K):
        weights = weights + jnp.where(lane == lpos_ref[:, k:k + 1], gate_ref[:, k:k + 1], 0.0)
    ys = ys2_ref[...].astype(BF16)
    f = jnp.dot(weights.astype(BF16), ys, preferred_element_type=F32)
    o_ref[...] = x_ref[...] + mod_ref[0, 5:6, :] * _rms(f, g_ref[...])


def _combine(yb, lpos, gate, x2, mod, g_post, plan, batch):
    cnt_s, loff_s, gdst_s = plan[3:6]
    t, d = x2.shape
    tm = min(MOE_TILE, t)
    rows = TOP_K * tm
    tiles_per_mod = (t // batch) // tm
    return pl.pallas_call(
        functools.partial(_combine_body, n_tiles=t // tm),
        out_shape=jax.ShapeDtypeStruct((t, d), F32),
        grid_spec=pltpu.PrefetchScalarGridSpec(
            num_scalar_prefetch=3,
            grid=(t // tm,),
            in_specs=[pl.BlockSpec(memory_space=pl.ANY),
                      pl.BlockSpec((tm, TOP_K), lambda i, *_: (i, 0)),
                      pl.BlockSpec((tm, TOP_K), lambda i, *_: (i, 0)),
                      pl.BlockSpec((tm, d), lambda i, *_: (i, 0)),
                      pl.BlockSpec((1, 8, d), lambda i, *_: (i // tiles_per_mod, 0, 0)),
                      pl.BlockSpec((1, d), lambda i, *_: (0, 0))],
            out_specs=pl.BlockSpec((tm, d), lambda i, *_: (i, 0)),
            scratch_shapes=[pltpu.VMEM((rows, 1, d), F32), pltpu.VMEM((rows, 1, d), F32), pltpu.VMEM((rows, d), F32),
                            pltpu.SemaphoreType.DMA, pltpu.SemaphoreType.DMA],
        ),
        compiler_params=_cparams(("arbitrary",)),
        name="moe_combine",
    )(cnt_s, loff_s, gdst_s, yb, lpos, gate, x2, mod, g_post.reshape(1, d))


def _moe_ffn(x2, mod, g_pre, g_post, rw, rb, w_gu, b_gu, w_down, b_down, batch):
    h, lpos, gate, cnt, cbase = _route(x2, mod, g_pre, rw, rb, batch)
    plan = _moe_plan(cnt, cbase, x2.shape[0] * TOP_K)
    xb = _dispatch(h, lpos, plan)
    yb = _experts(xb, plan, w_gu, b_gu, w_down, b_down)
    return _combine(yb, lpos, gate, x2, mod, g_post, plan, batch)


CONF_HALO = 16


def _odd_body(a_ref, ap_ref, an_ref, gb_ref, cu_ref, cup_ref, cun_ref, x_ref, mod_ref, dw_ref, dwb_ref, lng_ref,
              lnb_ref, sc_ref, w_ref, gpost_ref, o_ref, *, tiles_per_seq):
    i = pl.program_id(0)
    tm = x_ref.shape[0]
    first = (i % tiles_per_seq) == 0
    last = (i % tiles_per_seq) == tiles_per_seq - 1

    def glu(ref):
        v = ref[...].astype(F32)
        return v[:, :CONF_W] * jax.nn.sigmoid(v[:, CONF_W:])

    pad = CONF_KERNEL // 2
    ext = jnp.concatenate([jnp.where(first, 0.0, glu(ap_ref)), glu(a_ref), jnp.where(last, 0.0, glu(an_ref))], axis=0)
    acc = jnp.zeros((tm, CONF_W), F32) + dwb_ref[...]
    for k in range(CONF_KERNEL):
        lo = CONF_HALO - pad + k
        acc = acc + dw_ref[k:k + 1, :] * ext[lo:lo + tm, :]
    mu = jnp.mean(acc, axis=-1, keepdims=True)
    xc = acc - mu
    hn = xc * lax.rsqrt(jnp.mean(xc * xc, axis=-1, keepdims=True) + EPS) * lng_ref[...] + lnb_ref[...]
    hc = (hn * jax.nn.sigmoid(hn)).astype(BF16)

    def gated(ref):
        v = ref[...].astype(F32)
        return v[:, :SCONV_W] * v[:, SCONV_W:]

    zc = gated(cu_ref)
    zext = jnp.concatenate([jnp.where(first, 0.0, gated(cup_ref)), zc, jnp.where(last, 0.0, gated(cun_ref))], axis=0)
    z = sc_ref[0:1, :] * zext[0:tm] + sc_ref[1:2, :] * zc + sc_ref[2:3, :] * zext[2 * GRID_W:2 * GRID_W + tm]
    z = (gb_ref[...].astype(F32) * z).astype(BF16)
    y = jnp.dot(hc, w_ref[:CONF_W, :], preferred_element_type=F32) + jnp.dot(z, w_ref[CONF_W:, :],
                                                                            preferred_element_type=F32)
    o_ref[...] = x_ref[...] + mod_ref[0, 2:3, :] * _rms(y, gpost_ref[...])


def _odd_mix(a, gb, cu, x2, mod, conf_dw, conf_dw_b, ln_g, ln_b, sconv, w_out, g_post, batch):
    t, d = x2.shape
    seq = t // batch
    tm = min(ROW_TILE, seq)
    tiles_per_seq = seq // tm
    nt = t // tm
    hb_c, hb_s = tm // CONF_HALO, tm // GRID_W
    row = lambda w: pl.BlockSpec((tm, w), lambda i: (i, 0))
    prev = lambda rows, per, w: pl.BlockSpec((rows, w), lambda i: (jnp.maximum(i * per - 1, 0), 0))
    nxt = lambda rows, per, w: pl.BlockSpec((rows, w), lambda i: (jnp.minimum((i + 1) * per, nt * per - 1), 0))
    full = lambda arr: pl.BlockSpec(arr.shape, lambda i: (0,) * arr.ndim)
    smalls = [conf_dw, conf_dw_b.reshape(1, -1), ln_g.reshape(1, -1), ln_b.reshape(1, -1), sconv, w_out,
              g_post.reshape(1, d)]
    return pl.pallas_call(
        functools.partial(_odd_body, tiles_per_seq=tiles_per_seq),
        out_shape=jax.ShapeDtypeStruct((t, d), F32),
        grid=(nt,),
        in_specs=[row(2 * CONF_W), prev(CONF_HALO, hb_c, 2 * CONF_W), nxt(CONF_HALO, hb_c, 2 * CONF_W),
                  row(SCONV_W),
                  row(2 * SCONV_W), prev(GRID_W, hb_s, 2 * SCONV_W), nxt(GRID_W, hb_s, 2 * SCONV_W),
                  row(d), pl.BlockSpec((1, 8, d), lambda i: (i // tiles_per_seq, 0, 0))] + [full(s) for s in smalls],
        out_specs=row(d),
        compiler_params=_cparams(("parallel",)),
        name="odd_mix",
    )(a, a, a, gb, cu, cu, cu, x2, mod, *smalls)


COND_ROWS = 16


def _mod_rows(mods_layer):
    m = mods_layer.reshape(COND_ROWS, 6, D_MODEL)
    return jnp.concatenate([m, jnp.zeros((COND_ROWS, 2, D_MODEL), F32)], axis=1)


def _even_weights(w_in, gk_up, gk_b):
    cuts = np.cumsum((GLA_KW, GLA_KW, GLA_VW, GLA_VW, GLA_LOWRANK, GLA_LOWRANK, FNET_W))[:-1]
    wq, wk, wv, wr, wlf, wlb, wf = jnp.split(w_in, [int(v) for v in cuts], axis=-1)
    w_main = jnp.concatenate([wq, wk, wv, wr, wf], axis=-1).astype(BF16)
    pad = jnp.zeros((w_in.shape[0], LANES - 2 * GLA_LOWRANK), w_in.dtype)
    w_tail = jnp.concatenate([wlf, wlb, pad], axis=-1).astype(BF16)
    up_pad = jnp.zeros((2, LANES, GLA_KW), F32)
    up_pad = up_pad.at[0, :GLA_LOWRANK].set(gk_up[0]).at[1, GLA_LOWRANK:2 * GLA_LOWRANK].set(gk_up[1])
    return w_main, w_tail, up_pad, gk_b.reshape(2, 1, GLA_KW)


EVEN_WIDTHS = (GLA_KW, GLA_KW, GLA_VW, GLA_VW, FNET_W)


def _even_gla(x2, ctx2, mod, g_pre, w_main, w_tail, up_pad, bias, batch):
    nb = batch
    cq, ck, cv, _, _, clr = _norm_proj(ctx2, mod[nb:nb + 1], ctx2.shape[0], g_pre, w_main, EVEN_WIDTHS, w_tail)
    zero = jnp.zeros((nb, 2, GLA_HEADS, GLA_DV, GLA_DK), F32)
    _, _, s_ctx = _gla(cq, ck, cv, clr, up_pad, bias, zero, nb)
    q, k, v, r, f, lr = _norm_proj(x2, mod[:nb], x2.shape[0] // nb, g_pre, w_main, EVEN_WIDTHS, w_tail)
    o_f, o_b, _ = _gla(q, k, v, lr, up_pad, bias, s_ctx, nb)
    return r, f, o_f, o_b


def kernel(x, c, ctx, c_ctx, mod_w, mod_b, norm_mix_pre, norm_mix_post, norm_ffn_pre, norm_ffn_post, ev_w_in, ev_gk_up, ev_gk_b, ev_gnorm, ev_w_out, od_w_in, od_conf_dw, od_conf_dw_b, od_conf_ln_g, od_conf_ln_b, od_sconv, od_w_out, router_w, router_b, exp_w_gu, exp_b_gu, exp_w_down, exp_b_down):
    nb, seq, d = x.shape
    cond = jnp.concatenate([c, c_ctx[None], jnp.zeros((COND_ROWS - nb - 1, d), F32)], axis=0)
    mods = _adaln(cond, mod_w, mod_b)
    x2 = x.reshape(nb * seq, d)
    ctx2 = ctx.reshape(nb * ctx.shape[1], d)
    depth = mod_w.shape[0]
    assert depth == 2, "layer pattern implemented for one even layer followed by one odd layer"
    mod = _mod_rows(mods[0])
    wm, wt, up, bias = _even_weights(ev_w_in[0], ev_gk_up[0], ev_gk_b[0])
    r, f, o_f, o_b = _even_gla(x2, ctx2, mod, norm_mix_pre[0], wm, wt, up, bias, nb)
    fo = _fourier_mix(f, nb)
    x2 = _even_out(o_f, o_b, r, fo, x2, mod, ev_gnorm[0], ev_w_out[0].astype(BF16), norm_mix_post[0], nb)
    x2 = _moe_ffn(x2, mod, norm_ffn_pre[0], norm_ffn_post[0], router_w[0], router_b[0],
                  exp_w_gu[0], exp_b_gu[0], exp_w_down[0], exp_b_down[0], nb)
    mod = _mod_rows(mods[1])
    a, gb, cu = _norm_proj(x2, mod[:nb], seq, norm_mix_pre[1], od_w_in[0].astype(BF16),
                           (2 * CONF_W, SCONV_W, 2 * SCONV_W))
    x2 = _odd_mix(a, gb, cu, x2, mod, od_conf_dw[0], od_conf_dw_b[0], od_conf_ln_g[0], od_conf_ln_b[0], od_sconv[0],
                  od_w_out[0].astype(BF16), norm_mix_post[1], nb)
    x2 = _moe_ffn(x2, mod, norm_ffn_pre[1], norm_ffn_post[1], router_w[1], router_b[1],
                  exp_w_gu[1], exp_b_gu[1], exp_w_down[1], exp_b_down[1], nb)
    return x2.reshape(nb, seq, d)
```

```python
import functools

import numpy as np
import jax
import jax.numpy as jnp
from jax import lax
from jax.experimental import pallas as pl
from jax.experimental.pallas import tpu as pltpu

F32 = jnp.float32
BF16 = jnp.bfloat16

D_MODEL = 1024
EPS = 1e-6
GRID_W = 64
GLA_HEADS = 4
GLA_DK = 128
GLA_DV = 256
GLA_LOWRANK = 16
GLA_GATE_NORM = 16.0
GLA_KW = GLA_HEADS * GLA_DK
GLA_VW = GLA_HEADS * GLA_DV
FNET_GROUPS = 4
FNET_GROUP_W = 128
FNET_W = FNET_GROUPS * FNET_GROUP_W
CONF_W = 512
CONF_KERNEL = 31
SCONV_W = 512
N_EXPERTS = 32
TOP_K = 4
D_FF = D_MODEL
SWIGLU_LIMIT = 7.0
SWIGLU_ALPHA = 1.702

LANES = 128
SUBLANES = 8
VMEM_LIMIT = 56 * 1024 * 1024

GLA_CHUNK = 128
ROW_TILE = 512


def _cparams(sem):
    return pltpu.CompilerParams(dimension_semantics=sem, vmem_limit_bytes=VMEM_LIMIT)


def _adaln_body(c_ref, w_ref, b_ref, o_ref):
    c = c_ref[...]
    s = c * jax.nn.sigmoid(c)
    o_ref[0] = jnp.dot(s, w_ref[0], precision=lax.Precision.HIGHEST, preferred_element_type=F32) + b_ref[0]


def _adaln(cond, mod_w, mod_b):
    depth, d, n = mod_w.shape
    rows = cond.shape[0]
    tn = 1536
    return pl.pallas_call(
        _adaln_body,
        out_shape=jax.ShapeDtypeStruct((depth, rows, n), F32),
        grid=(depth, n // tn),
        in_specs=[pl.BlockSpec((rows, d), lambda l, j: (0, 0)),
                  pl.BlockSpec((1, d, tn), lambda l, j: (l, 0, j)),
                  pl.BlockSpec((1, 1, tn), lambda l, j: (l, 0, j))],
        out_specs=pl.BlockSpec((1, rows, tn), lambda l, j: (l, 0, j)),
        compiler_params=_cparams(("arbitrary", "arbitrary")),
        name="adaln",
    )(cond, mod_w, mod_b.reshape(depth, 1, n))


def _modnorm(x, g, mod_ref, shift_row, scale_row):
    ms = jnp.mean(x * x, axis=-1, keepdims=True)
    y = x * lax.rsqrt(ms + EPS) * g
    return y * (1.0 + mod_ref[0, scale_row:scale_row + 1, :]) + mod_ref[0, shift_row:shift_row + 1, :]


def _normproj_body(x_ref, mod_ref, g_ref, w_ref, *rest, widths, has_f32_tail):
    if has_f32_tail:
        wt_ref, outs = rest[0], rest[1:]
    else:
        wt_ref, outs = None, rest
    h = _modnorm(x_ref[...], g_ref[...], mod_ref, 0, 1).astype(BF16)
    col = 0
    for o_ref, wd in zip(outs, widths):
        step = 512
        for j in range(0, wd, step):
            o_ref[:, j:j + step] = jnp.dot(h, w_ref[:, col + j:col + j + step],
                                           preferred_element_type=F32).astype(o_ref.dtype)
        col += wd
    if has_f32_tail:
        outs[-1][...] = jnp.dot(h, wt_ref[...], preferred_element_type=F32)


def _norm_proj(x2, mod, rows_per_mod, g, w, widths, w_tail=None):
    t, d = x2.shape
    tm = min(ROW_TILE, t)
    n = w.shape[1]
    assert sum(widths) == n and all(wd % 512 == 0 for wd in widths) and rows_per_mod % tm == 0
    tiles_per_mod = rows_per_mod // tm
    in_specs = [pl.BlockSpec((tm, d), lambda i: (i, 0)),
                pl.BlockSpec((1, 8, d), lambda i: (i // tiles_per_mod, 0, 0)),
                pl.BlockSpec((1, d), lambda i: (0, 0)),
                pl.BlockSpec((d, n), lambda i: (0, 0))]
    args = [x2, mod, g.reshape(1, d), w]
    out_shape = [jax.ShapeDtypeStruct((t, wd), BF16) for wd in widths]
    out_specs = [pl.BlockSpec((tm, wd), lambda i: (i, 0)) for wd in widths]
    if w_tail is not None:
        in_specs.append(pl.BlockSpec((d, LANES), lambda i: (0, 0)))
        args.append(w_tail)
        out_shape.append(jax.ShapeDtypeStruct((t, LANES), F32))
        out_specs.append(pl.BlockSpec((tm, LANES), lambda i: (i, 0)))
    return pl.pallas_call(
        functools.partial(_normproj_body, widths=tuple(widths), has_f32_tail=w_tail is not None),
        out_shape=out_shape,
        grid=(t // tm,),
        in_specs=in_specs,
        out_specs=out_specs,
        compiler_params=_cparams(("parallel",)),
        name="norm_proj",
    )(*args)


def _block_row(x, blk, r):
    c, w = x.shape
    x3 = x.reshape(c // blk, blk, w)
    return jnp.broadcast_to(x3[:, r:r + 1, :], x3.shape).reshape(c, w)


def _gla_direction(q, k, v, a, st, rev):
    c = q.shape[0]
    row = lax.broadcasted_iota(jnp.int32, (c, 1), 0)
    ri = lax.broadcasted_iota(jnp.int32, (c, c), 0)
    ci = lax.broadcasted_iota(jnp.int32, (c, c), 1)
    qk = jnp.sum(q * k, axis=-1, keepdims=True)
    amat = jnp.where(ri == ci, qk, 0.0)
    pq = a
    pk = jnp.ones_like(a)
    half = 1
    while half < c:
        blk = 2 * half
        in_second = (row & half) != 0
        q_side = in_second if not rev else jnp.logical_not(in_second)
        qs = jnp.where(q_side, q * pq, 0.0).astype(BF16)
        ks = jnp.where(q_side, 0.0, k * pk).astype(BF16)
        s = lax.dot_general(qs, ks, (((1,), (1,)), ((), ())), preferred_element_type=F32)
        amat = amat + jnp.where((ri // blk) == (ci // blk), s, 0.0)
        if blk >= SUBLANES:
            if not rev:
                t_near = _block_row(pq, blk, half - 1)
                t_far = _block_row(pq, blk, blk - 1)
            else:
                t_near = _block_row(pq, blk, half)
                t_far = _block_row(pq, blk, 0)
        else:
            t_near = jnp.zeros_like(pq)
            t_far = jnp.zeros_like(pq)
            pos = row & (blk - 1)
            for u in range(blk):
                if (u >= half) != rev:
                    src = (half - 1) if not rev else half
                    t_near = jnp.where(pos == u, pltpu.roll(pq, (u - src) % c, 0), t_near)
                else:
                    src = (blk - 1) if not rev else 0
                    t_far = jnp.where(pos == u, pltpu.roll(pq, (u - src) % c, 0), t_far)
        pq_new = jnp.where(q_side, pq * t_near, pq)
        pk = jnp.where(q_side, pk, pk * t_far)
        pq = pq_new
        half = blk
    o = jnp.dot(amat.astype(BF16), v, preferred_element_type=F32)
    o = o + lax.dot_general((q * pq).astype(BF16), st.astype(BF16), (((1,), (1,)), ((), ())),
                            preferred_element_type=F32)
    total = pq[c - 1:c, :] if not rev else pq[0:1, :]
    kv = lax.dot_general(v, (k * pk).astype(BF16), (((0,), (0,)), ((), ())), preferred_element_type=F32)
    return o, st * total + kv


GLA_HEADS_PER_STEP = 2


def _gla_body(qf_ref, kf_ref, vf_ref, lf_ref, qb_ref, kb_ref, vb_ref, lb_ref, up_ref, bias_ref, s0_ref,
              of_ref, ob_ref, sout_ref, st_ref):
    j = pl.program_id(2)

    @pl.when(j == 0)
    def _():
        st_ref[...] = s0_ref[0]

    scale = GLA_DK ** -0.5
    for d, (q_ref, k_ref, v_ref, l_ref, o_ref) in enumerate(((qf_ref, kf_ref, vf_ref, lf_ref, of_ref),
                                                             (qb_ref, kb_ref, vb_ref, lb_ref, ob_ref))):
        x = jnp.dot(l_ref[...], up_ref[d], precision=lax.Precision.HIGHEST, preferred_element_type=F32) + bias_ref[d]
        g = (jnp.minimum(x, 0.0) - jnp.log(1.0 + jnp.exp(-jnp.abs(x)))) * (1.0 / GLA_GATE_NORM)
        a = jnp.exp(g)
        for hh in range(GLA_HEADS_PER_STEP):
            ks = slice(hh * GLA_DK, (hh + 1) * GLA_DK)
            vs = slice(hh * GLA_DV, (hh + 1) * GLA_DV)
            q = q_ref[:, ks].astype(F32) * scale
            k = k_ref[:, ks].astype(F32)
            o, st_new = _gla_direction(q, k, v_ref[:, vs], a[:, ks], st_ref[d, hh], rev=(d == 1))
            o_ref[:, vs] = o.astype(o_ref.dtype)
            st_ref[d, hh] = st_new

    @pl.when(j == pl.num_programs(2) - 1)
    def _():
        sout_ref[0] = st_ref[...]


def _gla(q, k, v, lr, up_pad, bias, s0, batch):
    t = q.shape[0]
    seq = t // batch
    c = min(GLA_CHUNK, seq)
    nc = seq // c
    hps = GLA_HEADS_PER_STEP
    kw, vw = hps * GLA_DK, hps * GLA_DV
    fwd = lambda b, h, j: (b * nc + j, h)
    bwd = lambda b, h, j: (b * nc + nc - 1 - j, h)
    fwd0 = lambda b, h, j: (b * nc + j, 0)
    bwd0 = lambda b, h, j: (b * nc + nc - 1 - j, 0)
    st_spec = pl.BlockSpec((1, 2, hps, GLA_DV, GLA_DK), lambda b, h, j: (b, 0, h, 0, 0))
    return pl.pallas_call(
        _gla_body,
        out_shape=[jax.ShapeDtypeStruct((t, GLA_VW), BF16), jax.ShapeDtypeStruct((t, GLA_VW), BF16),
                   jax.ShapeDtypeStruct(s0.shape, F32)],
        grid=(batch, GLA_HEADS // hps, nc),
        in_specs=[pl.BlockSpec((c, kw), fwd), pl.BlockSpec((c, kw), fwd), pl.BlockSpec((c, vw), fwd),
                  pl.BlockSpec((c, LANES), fwd0),
                  pl.BlockSpec((c, kw), bwd), pl.BlockSpec((c, kw), bwd), pl.BlockSpec((c, vw), bwd),
                  pl.BlockSpec((c, LANES), bwd0),
                  pl.BlockSpec((2, LANES, kw), lambda b, h, j: (0, 0, h)),
                  pl.BlockSpec((2, 1, kw), lambda b, h, j: (0, 0, h)),
                  st_spec],
        out_specs=[pl.BlockSpec((c, vw), fwd), pl.BlockSpec((c, vw), bwd), st_spec],
        scratch_shapes=[pltpu.VMEM((2, hps, GLA_DV, GLA_DK), F32)],
        compiler_params=_cparams(("parallel", "parallel", "arbitrary")),
        name="gla_scan",
    )(q, k, v, lr, q, k, v, lr, up_pad, bias, s0)


FFT_P = 128


def _fft_tables(seq):
    q = seq // FFT_P
    n_hi = np.arange(q)
    ang1 = 2.0 * np.pi * ((n_hi[:, None] * n_hi[None, :]) % q) / q
    w1 = np.concatenate([np.cos(ang1), -np.sin(ang1)], axis=0)
    n_lo = np.arange(FFT_P)
    k_a = np.arange(FFT_P)
    k_b = np.arange(q)
    k_full = k_b[:, None, None] + q * k_a[None, :, None]
    ang2 = 2.0 * np.pi * ((k_full * n_lo[None, None, :]) % seq) / seq
    c2, s2 = np.cos(ang2), np.sin(ang2)
    w2 = np.concatenate([np.concatenate([c2, s2], axis=2), np.concatenate([-s2, c2], axis=2)], axis=1)
    ch = np.arange(FNET_GROUP_W)
    angc = 2.0 * np.pi * ((ch[:, None] * ch[None, :]) % FNET_GROUP_W) / FNET_GROUP_W
    scale = 1.0 / np.sqrt(float(seq) * FNET_GROUP_W)
    eye = np.eye(FNET_GROUPS)
    wc = np.concatenate([np.kron(eye, np.cos(angc)), np.kron(eye, np.sin(angc))], axis=0) * scale
    return (jnp.asarray(w1, BF16), jnp.asarray(w2, BF16), jnp.asarray(wc, BF16))


def _fft1_body(z_ref, w_ref, o_ref):
    o_ref[0] = jnp.dot(w_ref[...], z_ref[0], preferred_element_type=F32).astype(o_ref.dtype)


def _fft2_body(a_ref, w_ref, wc_ref, o_ref):
    a = a_ref[0, :, 0].reshape(2 * FFT_P, FNET_W)
    z = jnp.dot(w_ref[0], a, preferred_element_type=F32).astype(BF16)
    zz = jnp.concatenate([z[:FFT_P], z[FFT_P:]], axis=-1)
    o_ref[0] = jnp.dot(zz, wc_ref[...], preferred_element_type=F32).astype(o_ref.dtype)


def _fourier_mix(f, batch):
    t = f.shape[0]
    seq = t // batch
    q = seq // FFT_P
    w1, w2, wc = _fft_tables(seq)
    cols = FFT_P * FNET_W
    tn = min(8192, cols)
    z = f.reshape(batch, q, cols)
    a = pl.pallas_call(
        _fft1_body,
        out_shape=jax.ShapeDtypeStruct((batch, 2 * q, cols), BF16),
        grid=(batch, cols // tn),
        in_specs=[pl.BlockSpec((1, q, tn), lambda b, j: (b, 0, j)),
                  pl.BlockSpec((2 * q, q), lambda b, j: (0, 0))],
        out_specs=pl.BlockSpec((1, 2 * q, tn), lambda b, j: (b, 0, j)),
        compiler_params=_cparams(("parallel", "parallel")),
        name="fft_stage1",
    )(z, w1)
    a = a.reshape(batch, 2, q, FFT_P, FNET_W)
    out = pl.pallas_call(
        _fft2_body,
        out_shape=jax.ShapeDtypeStruct((batch, FFT_P, q * FNET_W), BF16),
        grid=(batch, q),
        in_specs=[pl.BlockSpec((1, 2, 1, FFT_P, FNET_W), lambda b, kb: (b, 0, kb, 0, 0)),
                  pl.BlockSpec((1, 2 * FFT_P, 2 * FFT_P), lambda b, kb: (kb, 0, 0)),
                  pl.BlockSpec((2 * FNET_W, FNET_W), lambda b, kb: (0, 0))],
        out_specs=pl.BlockSpec((1, FFT_P, FNET_W), lambda b, kb: (b, 0, kb)),
        compiler_params=_cparams(("parallel", "parallel")),
        name="fft_stage2",
    )(a, w2, wc)
    return out.reshape(batch, FFT_P, q, FNET_W).reshape(t, FNET_W)


def _rms(y, g):
    return y * lax.rsqrt(jnp.mean(y * y, axis=-1, keepdims=True) + EPS) * g


def _even_out_body(of_ref, ob_ref, r_ref, fo_ref, x_ref, mod_ref, gn_ref, w_ref, gpost_ref, o_ref):
    parts = []
    for h in range(GLA_HEADS):
        sl = slice(h * GLA_DV, (h + 1) * GLA_DV)
        o = of_ref[:, sl].astype(F32) + ob_ref[:, sl].astype(F32)
        r = r_ref[:, sl].astype(F32)
        parts.append((_rms(o, gn_ref[...]) * (r * jax.nn.sigmoid(r))).astype(BF16))
    y = jnp.dot(jnp.concatenate(parts, axis=-1), w_ref[:GLA_VW, :], preferred_element_type=F32)
    y = y + jnp.dot(fo_ref[...], w_ref[GLA_VW:, :], preferred_element_type=F32)
    o_ref[...] = x_ref[...] + mod_ref[0, 2:3, :] * _rms(y, gpost_ref[...])


def _even_out(o_f, o_b, r, fo, x2, mod, gnorm, w_out, g_post, batch):
    t, d = x2.shape
    tm = min(ROW_TILE, t)
    tiles_per_mod = (t // batch) // tm
    row = lambda w: pl.BlockSpec((tm, w), lambda i: (i, 0))
    full = lambda a: pl.BlockSpec(a.shape, lambda i: (0,) * a.ndim)
    gn = gnorm.reshape(1, GLA_DV)
    gp = g_post.reshape(1, d)
    return pl.pallas_call(
        _even_out_body,
        out_shape=jax.ShapeDtypeStruct((t, d), F32),
        grid=(t // tm,),
        in_specs=[row(GLA_VW), row(GLA_VW), row(GLA_VW), row(FNET_W), row(d),
                  pl.BlockSpec((1, 8, d), lambda i: (i // tiles_per_mod, 0, 0)),
                  full(gn), full(w_out), full(gp)],
        out_specs=row(d),
        compiler_params=_cparams(("parallel",)),
        name="even_out",
    )(o_f, o_b, r, fo, x2, mod, gn, w_out, gp)


MOE_TILE = 512
MOE_BLOCK = 512
NEG_BIG = -1e30


def _route_body(x_ref, mod_ref, g_ref, rw_ref, rb_ref, h_ref, lpos_ref, gate_ref, cnt_ref, cbase_ref, base_ref):
    i = pl.program_id(0)

    @pl.when(i == 0)
    def _():
        base_ref[...] = jnp.zeros_like(base_ref)

    tm = x_ref.shape[0]
    h = _modnorm(x_ref[...], g_ref[...], mod_ref, 3, 4)
    h_hi = h.astype(BF16)
    h_ref[...] = h_hi
    h_lo = (h - h_hi.astype(F32)).astype(BF16)
    both = jnp.dot(h_hi, rw_ref[...], preferred_element_type=F32)
    logits = (both[:, :LANES] + both[:, LANES:]
              + jnp.dot(h_lo, rw_ref[:, :LANES], preferred_element_type=F32) + rb_ref[...])
    lane = lax.broadcasted_iota(jnp.int32, (tm, LANES), 1).astype(F32)
    work = logits
    vals, hots = [], []
    for _k in range(TOP_K):
        m = jnp.max(work, axis=-1, keepdims=True)
        idx = jnp.min(jnp.where(work == m, lane, float(LANES)), axis=-1, keepdims=True)
        hot = lane == idx
        vals.append(m)
        hots.append(hot)
        work = jnp.where(hot, -jnp.inf, work)
    es = [jnp.exp(v - vals[0]) for v in vals]
    inv = 1.0 / (es[0] + es[1] + es[2] + es[3])
    sel = sum(hh.astype(F32) for hh in hots)
    ri = lax.broadcasted_iota(jnp.int32, (tm, tm), 0)
    ci = lax.broadcasted_iota(jnp.int32, (tm, tm), 1)
    earlier = jnp.where(ci < ri, 1.0, 0.0).astype(BF16)
    cum = jnp.dot(earlier, sel.astype(BF16), preferred_element_type=F32)
    cnt = jnp.sum(sel, axis=0, keepdims=True)
    ei = lax.broadcasted_iota(jnp.int32, (LANES, LANES), 0)
    ej = lax.broadcasted_iota(jnp.int32, (LANES, LANES), 1)
    before = jnp.where(ei < ej, 1.0, 0.0)
    loff = jnp.dot(jnp.broadcast_to(cnt, (SUBLANES, LANES)), before, precision=lax.Precision.HIGHEST,
                   preferred_element_type=F32)[0:1]
    where_to = cum + loff
    for k in range(TOP_K):
        p = jnp.sum(jnp.where(hots[k], where_to, 0.0), axis=-1, keepdims=True)
        lpos_ref[:, k:k + 1] = p.astype(jnp.int32)
        gate_ref[:, k:k + 1] = es[k] * inv
    cnt_ref[0] = cnt.astype(jnp.int32)
    cbase_ref[0] = base_ref[...].astype(jnp.int32)
    base_ref[...] = base_ref[...] + cnt


def _route(x2, mod, g_pre, rw, rb, batch):
    t, d = x2.shape
    tm = min(MOE_TILE, t)
    nt = t // tm
    tiles_per_mod = (t // batch) // tm
    rw_pad = jnp.zeros((d, LANES), F32).at[:, :N_EXPERTS].set(rw)
    rw_hi = rw_pad.astype(BF16)
    rw_pad = jnp.concatenate([rw_hi, (rw_pad - rw_hi.astype(F32)).astype(BF16)], axis=1)
    rb_pad = jnp.full((1, LANES), NEG_BIG, F32).at[0, :N_EXPERTS].set(rb)
    return pl.pallas_call(
        _route_body,
        out_shape=[jax.ShapeDtypeStruct((t, d), BF16), jax.ShapeDtypeStruct((t, TOP_K), jnp.int32),
                   jax.ShapeDtypeStruct((t, TOP_K), F32), jax.ShapeDtypeStruct((nt, 1, LANES), jnp.int32),
                   jax.ShapeDtypeStruct((nt, 1, LANES), jnp.int32)],
        grid=(nt,),
        in_specs=[pl.BlockSpec((tm, d), lambda i: (i, 0)),
                  pl.BlockSpec((1, 8, d), lambda i: (i // tiles_per_mod, 0, 0)),
                  pl.BlockSpec((1, d), lambda i: (0, 0)),
                  pl.BlockSpec((d, 2 * LANES), lambda i: (0, 0)),
                  pl.BlockSpec((1, LANES), lambda i: (0, 0))],
        out_specs=[pl.BlockSpec((tm, d), lambda i: (i, 0)), pl.BlockSpec((tm, TOP_K), lambda i: (i, 0)),
                   pl.BlockSpec((tm, TOP_K), lambda i: (i, 0)), pl.BlockSpec((1, 1, LANES), lambda i: (i, 0, 0)),
                   pl.BlockSpec((1, 1, LANES), lambda i: (i, 0, 0))],
        scratch_shapes=[pltpu.VMEM((1, LANES), F32)],
        compiler_params=_cparams(("arbitrary",)),
        name="moe_route",
    )(x2, mod, g_pre.reshape(1, d), rw_pad, rb_pad)


def _moe_plan(cnt, cbase, n_slots):
    cnt = cnt[:, 0, :N_EXPERTS]
    cbase = cbase[:, 0, :N_EXPERTS]
    total = cbase[-1] + cnt[-1]
    padded = (total + MOE_BLOCK - 1) // MOE_BLOCK * MOE_BLOCK
    pad_end = jnp.cumsum(padded)
    start = pad_end - padded
    n_blocks = (n_slots + N_EXPERTS * (MOE_BLOCK - 1) + MOE_BLOCK - 1) // MOE_BLOCK
    n_used = pad_end[-1] // MOE_BLOCK
    blk = jnp.arange(n_blocks, dtype=jnp.int32)
    first_row = jnp.minimum(blk, n_used - 1) * MOE_BLOCK
    block_e = jnp.minimum(jnp.sum((pad_end[None, :] <= first_row[:, None]).astype(jnp.int32), axis=1), N_EXPERTS - 1)
    loff = jnp.cumsum(cnt, axis=1) - cnt
    gdst = start[None, :] + cbase
    flat = lambda a: a.reshape(-1).astype(jnp.int32)
    fill = (flat(padded - total), jnp.zeros((N_EXPERTS,), jnp.int32), flat(start + total))
    return (n_blocks, block_e, n_used.reshape(1).astype(jnp.int32), flat(cnt), flat(loff), flat(gdst)) + fill


def _strip_copies(cnt_s, loff_s, gdst_s, tile, local_ref, global_ref, sem, to_global, max_rows):
    def per_expert(e, carry):
        n = cnt_s[tile * N_EXPERTS + e]
        src0 = loff_s[tile * N_EXPERTS + e]
        dst0 = gdst_s[tile * N_EXPERTS + e]
        off = jnp.int32(0)
        size = max_rows
        while size >= 1:
            take = n & size

            @pl.when(take != 0)
            def _(off=off, size=size):
                loc = local_ref.at[pl.ds(src0 + off, size)]
                glo = global_ref.at[pl.ds(dst0 + off, size)]
                if to_global:
                    pltpu.make_async_copy(loc, glo, sem).start()
                else:
                    pltpu.make_async_copy(glo, loc, sem).start()

            off = off + take
            size //= 2
        return carry

    lax.fori_loop(0, N_EXPERTS, per_expert, 0)


def _dispatch_body(cnt_s, loff_s, gdst_s, zcnt_s, zoff_s, zdst_s, nu_s, h_ref, lpos_ref, xb_ref, xs0_ref, xs1_ref, zero_ref,
                   sem0, sem1, zsem, *, n_blocks, n_fill, n_tiles):
    i = pl.program_id(0)
    tm = h_ref.shape[0]
    rows = TOP_K * tm
    lane = lax.broadcasted_iota(jnp.int32, (tm, rows), 1)
    onehot = jnp.zeros((tm, rows), F32)
    for k in range(TOP_K):
        onehot = onehot + jnp.where(lane == lpos_ref[:, k:k + 1], 1.0, 0.0)
    xs = lax.dot_general(onehot.astype(BF16), h_ref[...], (((0,), (0,)), ((), ())), preferred_element_type=F32)
    slots = ((xs0_ref, sem0), (xs1_ref, sem1))

    def drain(slot):
        xs_ref, sem = slots[slot]
        pltpu.make_async_copy(xs_ref, xb_ref.at[pl.ds(0, rows)], sem).wait()

    for slot in range(2):
        @pl.when(i % 2 == slot)
        def _(slot=slot):
            @pl.when(i >= 2)
            def _():
                drain(slot)

            xs_ref, sem = slots[slot]
            xs_ref[...] = xs.reshape(rows, 1, xs.shape[-1])
            _strip_copies(cnt_s, loff_s, gdst_s, i, xs_ref, xb_ref, sem, True, tm)

    @pl.when(i == n_tiles - 1)
    def _():
        drain((n_tiles - 1) % 2)
        if n_tiles >= 2:
            drain(n_tiles % 2)
        zero_ref[...] = jnp.zeros_like(zero_ref)
        _strip_copies(zcnt_s, zoff_s, zdst_s, 0, zero_ref, xb_ref, zsem, True, MOE_BLOCK // 2)

        def per_block(b, carry):
            @pl.when(b >= nu_s[0])
            def _():
                pltpu.make_async_copy(zero_ref, xb_ref.at[pl.ds(b * MOE_BLOCK, MOE_BLOCK)], zsem).start()
            return carry

        lax.fori_loop(0, n_blocks, per_block, 0)
        pltpu.make_async_copy(xb_ref.at[pl.ds(0, n_fill)], xb_ref.at[pl.ds(0, n_fill)], zsem).wait()


def _dispatch(h, lpos, plan):
    n_blocks, _, n_used, cnt_s, loff_s, gdst_s, zcnt_s, zoff_s, zdst_s = plan
    t, d = h.shape
    tm = min(MOE_TILE, t)
    rows = TOP_K * tm
    n_fill = n_blocks * MOE_BLOCK - TOP_K * t
    return pl.pallas_call(
        functools.partial(_dispatch_body, n_blocks=n_blocks, n_fill=n_fill, n_tiles=t // tm),
        out_shape=jax.ShapeDtypeStruct((n_blocks * MOE_BLOCK, 1, d), F32),
        grid_spec=pltpu.PrefetchScalarGridSpec(
            num_scalar_prefetch=7,
            grid=(t // tm,),
            in_specs=[pl.BlockSpec((tm, d), lambda i, *_: (i, 0)),
                      pl.BlockSpec((tm, TOP_K), lambda i, *_: (i, 0))],
            out_specs=pl.BlockSpec(memory_space=pl.ANY),
            scratch_shapes=[pltpu.VMEM((rows, 1, d), F32), pltpu.VMEM((rows, 1, d), F32),
                            pltpu.VMEM((MOE_BLOCK, 1, d), F32),
                            pltpu.SemaphoreType.DMA, pltpu.SemaphoreType.DMA, pltpu.SemaphoreType.DMA],
        ),
        compiler_params=_cparams(("arbitrary",)),
        name="moe_dispatch",
    )(cnt_s, loff_s, gdst_s, zcnt_s, zoff_s, zdst_s, n_used, h, lpos)


def _experts_body(be_s, nu_s, x_ref, wgu_ref, bgu_ref, wd_ref, bd_ref, o_ref, x2_ref, wgu_bf, wd_bf):
    i = pl.program_id(0)
    rows, _, d = x_ref.shape

    @pl.when(i < nu_s[0])
    def _():
        @pl.when((i == 0) | (be_s[i] != be_s[jnp.maximum(i - 1, 0)]))
        def _():
            wgu_bf[...] = wgu_ref[0].astype(BF16)
            wd_bf[...] = wd_ref[0].astype(BF16)

        x2_ref[...] = x_ref[...].reshape(rows, d)
        x = x2_ref[...].astype(BF16)
        gu = jnp.dot(x, wgu_bf[...], preferred_element_type=F32) + bgu_ref[0]
        g = jnp.minimum(gu[:, :D_FF], SWIGLU_LIMIT)
        u = jnp.clip(gu[:, D_FF:], -SWIGLU_LIMIT, SWIGLU_LIMIT)
        act = (u + 1.0) * (g * jax.nn.sigmoid(SWIGLU_ALPHA * g))
        y = jnp.dot(act.astype(BF16), wd_bf[...], preferred_element_type=F32) + bd_ref[0]
        o_ref[...] = y.reshape(rows, 1, d)

    @pl.when(i >= nu_s[0])
    def _():
        o_ref[...] = jnp.zeros_like(o_ref)


def _experts(xb, plan, layer, w_gu, b_gu, w_down, b_down):
    n_blocks, block_e, n_used = plan[:3]
    d = xb.shape[-1]
    base = layer * N_EXPERTS
    flat = lambda a: a.reshape((-1,) + a.shape[2:])
    return pl.pallas_call(
        _experts_body,
        out_shape=jax.ShapeDtypeStruct(xb.shape, F32),
        grid_spec=pltpu.PrefetchScalarGridSpec(
            num_scalar_prefetch=2,
            grid=(n_blocks,),
            in_specs=[pl.BlockSpec((MOE_BLOCK, 1, d), lambda i, be, nu: (i, 0, 0)),
                      pl.BlockSpec((1, d, 2 * D_FF), lambda i, be, nu: (base + be[i], 0, 0)),
                      pl.BlockSpec((1, 1, 2 * D_FF), lambda i, be, nu: (base + be[i], 0, 0)),
                      pl.BlockSpec((1, D_FF, d), lambda i, be, nu: (base + be[i], 0, 0)),
                      pl.BlockSpec((1, 1, d), lambda i, be, nu: (base + be[i], 0, 0))],
            out_specs=pl.BlockSpec((MOE_BLOCK, 1, d), lambda i, be, nu: (i, 0, 0)),
            scratch_shapes=[pltpu.VMEM((MOE_BLOCK, d), F32), pltpu.VMEM((d, 2 * D_FF), BF16),
                            pltpu.VMEM((D_FF, d), BF16)],
        ),
        compiler_params=_cparams(("arbitrary",)),
        name="moe_experts",
    )(block_e, n_used, xb, flat(w_gu), flat(b_gu)[:, None, :], flat(w_down), flat(b_down)[:, None, :])


def _combine_body(cnt_s, loff_s, gdst_s, yb_ref, lpos_ref, gate_ref, x_ref, mod_ref, g_ref, o_ref, ys0_ref, ys1_ref,
                  ys2_ref, sem0, sem1, *, n_tiles):
    i = pl.program_id(0)
    tm = x_ref.shape[0]
    rows = TOP_K * tm
    slots = ((ys0_ref, sem0), (ys1_ref, sem1))

    def fetch(tile, slot):
        ys_ref, sem = slots[slot]
        _strip_copies(cnt_s, loff_s, gdst_s, tile, ys_ref, yb_ref, sem, False, tm)

    @pl.when(i == 0)
    def _():
        fetch(0, 0)

    for slot in range(2):
        @pl.when(i % 2 == slot)
        def _(slot=slot):
            @pl.when(i + 1 < n_tiles)
            def _():
                fetch(i + 1, 1 - slot)

            ys_ref, sem = slots[slot]
            pltpu.make_async_copy(yb_ref.at[pl.ds(0, rows)], ys_ref, sem).wait()
            ys2_ref[...] = ys_ref[...].reshape(rows, ys_ref.shape[-1])

    lane = lax.broadcasted_iota(jnp.int32, (tm, rows), 1)
    weights = jnp.zeros((tm, rows), F32)
    for k in range(TOP_K):
        weights = weights + jnp.where(lane == lpos_ref[:, k:k + 1], gate_ref[:, k:k + 1], 0.0)
    ys = ys2_ref[...].astype(BF16)
    f = jnp.dot(weights.astype(BF16), ys, preferred_element_type=F32)
    o_ref[...] = x_ref[...] + mod_ref[0, 5:6, :] * _rms(f, g_ref[...])


def _combine(yb, lpos, gate, x2, mod, g_post, plan, batch):
    cnt_s, loff_s, gdst_s = plan[3:6]
    t, d = x2.shape
    tm = min(MOE_TILE, t)
    rows = TOP_K * tm
    tiles_per_mod = (t // batch) // tm
    return pl.pallas_call(
        functools.partial(_combine_body, n_tiles=t // tm),
        out_shape=jax.ShapeDtypeStruct((t, d), F32),
        grid_spec=pltpu.PrefetchScalarGridSpec(
            num_scalar_prefetch=3,
            grid=(t // tm,),
            in_specs=[pl.BlockSpec(memory_space=pl.ANY),
                      pl.BlockSpec((tm, TOP_K), lambda i, *_: (i, 0)),
                      pl.BlockSpec((tm, TOP_K), lambda i, *_: (i, 0)),
                      pl.BlockSpec((tm, d), lambda i, *_: (i, 0)),
                      pl.BlockSpec((1, 8, d), lambda i, *_: (i // tiles_per_mod, 0, 0)),
                      pl.BlockSpec((1, d), lambda i, *_: (0, 0))],
            out_specs=pl.BlockSpec((tm, d), lambda i, *_: (i, 0)),
            scratch_shapes=[pltpu.VMEM((rows, 1, d), F32), pltpu.VMEM((rows, 1, d), F32), pltpu.VMEM((rows, d), F32),
                            pltpu.SemaphoreType.DMA, pltpu.SemaphoreType.DMA],
        ),
        compiler_params=_cparams(("arbitrary",)),
        name="moe_combine",
    )(cnt_s, loff_s, gdst_s, yb, lpos, gate, x2, mod, g_post.reshape(1, d))


def _moe_ffn(x2, mod, g_pre, g_post, rw, rb, layer, w_gu, b_gu, w_down, b_down, batch):
    h, lpos, gate, cnt, cbase = _route(x2, mod, g_pre, rw, rb, batch)
    plan = _moe_plan(cnt, cbase, x2.shape[0] * TOP_K)
    xb = _dispatch(h, lpos, plan)
    yb = _experts(xb, plan, layer, w_gu, b_gu, w_down, b_down)
    return _combine(yb, lpos, gate, x2, mod, g_post, plan, batch)


CONF_HALO = 16


def _odd_body(a_ref, ap_ref, an_ref, gb_ref, cu_ref, cup_ref, cun_ref, x_ref, mod_ref, dw_ref, dwb_ref, lng_ref,
              lnb_ref, sc_ref, w_ref, gpost_ref, o_ref, shift_ref, *, tiles_per_seq):
    i = pl.program_id(0)
    tm = x_ref.shape[0]
    first = (i % tiles_per_seq) == 0
    last = (i % tiles_per_seq) == tiles_per_seq - 1

    def glu(ref):
        v = ref[...].astype(F32)
        return v[:, :CONF_W] * jax.nn.sigmoid(v[:, CONF_W:])

    pad = CONF_KERNEL // 2
    ext = jnp.concatenate([jnp.where(first, 0.0, glu(ap_ref)), glu(a_ref), jnp.where(last, 0.0, glu(an_ref))], axis=0)
    acc = jnp.zeros((tm, CONF_W), F32) + dwb_ref[...]
    span = shift_ref.shape[1]
    for r in range(SUBLANES):
        shift_ref[r] = ext[r:r + span, :]
    for k in range(CONF_KERNEL):
        lo = CONF_HALO - pad + k
        base = (lo // SUBLANES) * SUBLANES
        acc = acc + dw_ref[k:k + 1, :] * shift_ref[lo % SUBLANES, base:base + tm, :]
    mu = jnp.mean(acc, axis=-1, keepdims=True)
    xc = acc - mu
    hn = xc * lax.rsqrt(jnp.mean(xc * xc, axis=-1, keepdims=True) + EPS) * lng_ref[...] + lnb_ref[...]
    hc = (hn * jax.nn.sigmoid(hn)).astype(BF16)

    def gated(ref):
        v = ref[...].astype(F32)
        return v[:, :SCONV_W] * v[:, SCONV_W:]

    zc = gated(cu_ref)
    zext = jnp.concatenate([jnp.where(first, 0.0, gated(cup_ref)), zc, jnp.where(last, 0.0, gated(cun_ref))], axis=0)
    z = sc_ref[0:1, :] * zext[0:tm] + sc_ref[1:2, :] * zc + sc_ref[2:3, :] * zext[2 * GRID_W:2 * GRID_W + tm]
    z = (gb_ref[...].astype(F32) * z).astype(BF16)
    y = jnp.dot(hc, w_ref[:CONF_W, :], preferred_element_type=F32) + jnp.dot(z, w_ref[CONF_W:, :],
                                                                            preferred_element_type=F32)
    o_ref[...] = x_ref[...] + mod_ref[0, 2:3, :] * _rms(y, gpost_ref[...])


def _odd_mix(a, gb, cu, x2, mod, conf_dw, conf_dw_b, ln_g, ln_b, sconv, w_out, g_post, batch):
    t, d = x2.shape
    seq = t // batch
    tm = min(ROW_TILE, seq)
    tiles_per_seq = seq // tm
    nt = t // tm
    hb_c, hb_s = tm // CONF_HALO, tm // GRID_W
    row = lambda w: pl.BlockSpec((tm, w), lambda i: (i, 0))
    prev = lambda rows, per, w: pl.BlockSpec((rows, w), lambda i: (jnp.maximum(i * per - 1, 0), 0))
    nxt = lambda rows, per, w: pl.BlockSpec((rows, w), lambda i: (jnp.minimum((i + 1) * per, nt * per - 1), 0))
    full = lambda arr: pl.BlockSpec(arr.shape, lambda i: (0,) * arr.ndim)
    smalls = [conf_dw, conf_dw_b.reshape(1, -1), ln_g.reshape(1, -1), ln_b.reshape(1, -1), sconv, w_out,
              g_post.reshape(1, d)]
    return pl.pallas_call(
        functools.partial(_odd_body, tiles_per_seq=tiles_per_seq),
        out_shape=jax.ShapeDtypeStruct((t, d), F32),
        grid=(nt,),
        in_specs=[row(2 * CONF_W), prev(CONF_HALO, hb_c, 2 * CONF_W), nxt(CONF_HALO, hb_c, 2 * CONF_W),
                  row(SCONV_W),
                  row(2 * SCONV_W), prev(GRID_W, hb_s, 2 * SCONV_W), nxt(GRID_W, hb_s, 2 * SCONV_W),
                  row(d), pl.BlockSpec((1, 8, d), lambda i: (i // tiles_per_seq, 0, 0))] + [full(s) for s in smalls],
        out_specs=row(d),
        scratch_shapes=[pltpu.VMEM((SUBLANES, tm + CONF_HALO + SUBLANES, CONF_W), F32)],
        compiler_params=_cparams(("parallel",)),
        name="odd_mix",
    )(a, a, a, gb, cu, cu, cu, x2, mod, *smalls)


COND_ROWS = 16


def _mod_rows(mods_layer):
    m = mods_layer.reshape(COND_ROWS, 6, D_MODEL)
    return jnp.concatenate([m, jnp.zeros((COND_ROWS, 2, D_MODEL), F32)], axis=1)


def _even_weights(w_in, gk_up, gk_b):
    cuts = np.cumsum((GLA_KW, GLA_KW, GLA_VW, GLA_VW, GLA_LOWRANK, GLA_LOWRANK, FNET_W))[:-1]
    wq, wk, wv, wr, wlf, wlb, wf = jnp.split(w_in, [int(v) for v in cuts], axis=-1)
    w_main = jnp.concatenate([wq, wk, wv, wr, wf], axis=-1).astype(BF16)
    pad = jnp.zeros((w_in.shape[0], LANES - 2 * GLA_LOWRANK), w_in.dtype)
    w_tail = jnp.concatenate([wlf, wlb, pad], axis=-1).astype(BF16)
    up_pad = jnp.zeros((2, LANES, GLA_KW), F32)
    up_pad = up_pad.at[0, :GLA_LOWRANK].set(gk_up[0]).at[1, GLA_LOWRANK:2 * GLA_LOWRANK].set(gk_up[1])
    return w_main, w_tail, up_pad, gk_b.reshape(2, 1, GLA_KW)


EVEN_WIDTHS = (GLA_KW, GLA_KW, GLA_VW, GLA_VW, FNET_W)


def _even_gla(x2, ctx2, mod, g_pre, w_main, w_tail, up_pad, bias, batch):
    nb = batch
    cq, ck, cv, _, _, clr = _norm_proj(ctx2, mod[nb:nb + 1], ctx2.shape[0], g_pre, w_main, EVEN_WIDTHS, w_tail)
    zero = jnp.zeros((nb, 2, GLA_HEADS, GLA_DV, GLA_DK), F32)
    _, _, s_ctx = _gla(cq, ck, cv, clr, up_pad, bias, zero, nb)
    q, k, v, r, f, lr = _norm_proj(x2, mod[:nb], x2.shape[0] // nb, g_pre, w_main, EVEN_WIDTHS, w_tail)
    o_f, o_b, _ = _gla(q, k, v, lr, up_pad, bias, s_ctx, nb)
    return r, f, o_f, o_b


def kernel(x, c, ctx, c_ctx, mod_w, mod_b, norm_mix_pre, norm_mix_post, norm_ffn_pre, norm_ffn_post, ev_w_in, ev_gk_up, ev_gk_b, ev_gnorm, ev_w_out, od_w_in, od_conf_dw, od_conf_dw_b, od_conf_ln_g, od_conf_ln_b, od_sconv, od_w_out, router_w, router_b, exp_w_gu, exp_b_gu, exp_w_down, exp_b_down):
    nb, seq, d = x.shape
    cond = jnp.concatenate([c, c_ctx[None], jnp.zeros((COND_ROWS - nb - 1, d), F32)], axis=0)
    mods = _adaln(cond, mod_w, mod_b)
    x2 = x.reshape(nb * seq, d)
    ctx2 = ctx.reshape(nb * ctx.shape[1], d)
    depth = mod_w.shape[0]
    assert depth == 2, "layer pattern implemented for one even layer followed by one odd layer"
    mod = _mod_rows(mods[0])
    wm, wt, up, bias = _even_weights(ev_w_in[0], ev_gk_up[0], ev_gk_b[0])
    r, f, o_f, o_b = _even_gla(x2, ctx2, mod, norm_mix_pre[0], wm, wt, up, bias, nb)
    fo = _fourier_mix(f, nb)
    x2 = _even_out(o_f, o_b, r, fo, x2, mod, ev_gnorm[0], ev_w_out[0].astype(BF16), norm_mix_post[0], nb)
    x2 = _moe_ffn(x2, mod, norm_ffn_pre[0], norm_ffn_post[0], router_w[0], router_b[0],
                  0, exp_w_gu, exp_b_gu, exp_w_down, exp_b_down, nb)
    mod = _mod_rows(mods[1])
    a, gb, cu = _norm_proj(x2, mod[:nb], seq, norm_mix_pre[1], od_w_in[0].astype(BF16),
                           (2 * CONF_W, SCONV_W, 2 * SCONV_W))
    x2 = _odd_mix(a, gb, cu, x2, mod, od_conf_dw[0], od_conf_dw_b[0], od_conf_ln_g[0], od_conf_ln_b[0], od_sconv[0],
                  od_w_out[0].astype(BF16), norm_mix_post[1], nb)
    x2 = _moe_ffn(x2, mod, norm_ffn_pre[1], norm_ffn_post[1], router_w[1], router_b[1],
                  1, exp_w_gu, exp_b_gu, exp_w_down, exp_b_down, nb)
    return x2.reshape(nb, seq, d)
```

```python
import functools

import numpy as np
import jax
import jax.numpy as jnp
from jax import lax
from jax.experimental import pallas as pl
from jax.experimental.pallas import tpu as pltpu

F32 = jnp.float32
BF16 = jnp.bfloat16

D_MODEL = 1024
EPS = 1e-6
GRID_W = 64
GLA_HEADS = 4
GLA_DK = 128
GLA_DV = 256
GLA_LOWRANK = 16
GLA_GATE_NORM = 16.0
GLA_KW = GLA_HEADS * GLA_DK
GLA_VW = GLA_HEADS * GLA_DV
FNET_GROUPS = 4
FNET_GROUP_W = 128
FNET_W = FNET_GROUPS * FNET_GROUP_W
CONF_W = 512
CONF_KERNEL = 31
SCONV_W = 512
N_EXPERTS = 32
TOP_K = 4
D_FF = D_MODEL
SWIGLU_LIMIT = 7.0
SWIGLU_ALPHA = 1.702

LANES = 128
SUBLANES = 8
VMEM_LIMIT = 56 * 1024 * 1024

GLA_CHUNK = 128
ROW_TILE = 512


def _cparams(sem):
    return pltpu.CompilerParams(dimension_semantics=sem, vmem_limit_bytes=VMEM_LIMIT)


def _adaln_body(c_ref, w_ref, b_ref, o_ref):
    c = c_ref[...]
    s = c * jax.nn.sigmoid(c)
    o_ref[0] = jnp.dot(s, w_ref[0], precision=lax.Precision.HIGHEST, preferred_element_type=F32) + b_ref[0]


def _adaln(cond, mod_w, mod_b):
    depth, d, n = mod_w.shape
    rows = cond.shape[0]
    tn = 1536
    return pl.pallas_call(
        _adaln_body,
        out_shape=jax.ShapeDtypeStruct((depth, rows, n), F32),
        grid=(depth, n // tn),
        in_specs=[pl.BlockSpec((rows, d), lambda l, j: (0, 0)),
                  pl.BlockSpec((1, d, tn), lambda l, j: (l, 0, j)),
                  pl.BlockSpec((1, 1, tn), lambda l, j: (l, 0, j))],
        out_specs=pl.BlockSpec((1, rows, tn), lambda l, j: (l, 0, j)),
        compiler_params=_cparams(("arbitrary", "arbitrary")),
        name="adaln",
    )(cond, mod_w, mod_b.reshape(depth, 1, n))


def _modnorm(x, g, mod_ref, shift_row, scale_row):
    ms = jnp.mean(x * x, axis=-1, keepdims=True)
    y = x * lax.rsqrt(ms + EPS) * g
    return y * (1.0 + mod_ref[0, scale_row:scale_row + 1, :]) + mod_ref[0, shift_row:shift_row + 1, :]


def _normproj_body(x_ref, mod_ref, g_ref, w_ref, *rest, widths, has_f32_tail):
    if has_f32_tail:
        wt_ref, outs = rest[0], rest[1:]
    else:
        wt_ref, outs = None, rest
    h = _modnorm(x_ref[...], g_ref[...], mod_ref, 0, 1).astype(BF16)
    col = 0
    for o_ref, wd in zip(outs, widths):
        step = 512
        for j in range(0, wd, step):
            o_ref[:, j:j + step] = jnp.dot(h, w_ref[:, col + j:col + j + step],
                                           preferred_element_type=F32).astype(o_ref.dtype)
        col += wd
    if has_f32_tail:
        outs[-1][...] = jnp.dot(h, wt_ref[...], preferred_element_type=F32)


def _norm_proj(x2, mod, rows_per_mod, g, w, widths, w_tail=None):
    t, d = x2.shape
    tm = min(ROW_TILE, t)
    n = w.shape[1]
    assert sum(widths) == n and all(wd % 512 == 0 for wd in widths) and rows_per_mod % tm == 0
    tiles_per_mod = rows_per_mod // tm
    in_specs = [pl.BlockSpec((tm, d), lambda i: (i, 0)),
                pl.BlockSpec((1, 8, d), lambda i: (i // tiles_per_mod, 0, 0)),
                pl.BlockSpec((1, d), lambda i: (0, 0)),
                pl.BlockSpec((d, n), lambda i: (0, 0))]
    args = [x2, mod, g.reshape(1, d), w]
    out_shape = [jax.ShapeDtypeStruct((t, wd), BF16) for wd in widths]
    out_specs = [pl.BlockSpec((tm, wd), lambda i: (i, 0)) for wd in widths]
    if w_tail is not None:
        in_specs.append(pl.BlockSpec((d, LANES), lambda i: (0, 0)))
        args.append(w_tail)
        out_shape.append(jax.ShapeDtypeStruct((t, LANES), F32))
        out_specs.append(pl.BlockSpec((tm, LANES), lambda i: (i, 0)))
    return pl.pallas_call(
        functools.partial(_normproj_body, widths=tuple(widths), has_f32_tail=w_tail is not None),
        out_shape=out_shape,
        grid=(t // tm,),
        in_specs=in_specs,
        out_specs=out_specs,
        compiler_params=_cparams(("parallel",)),
        name="norm_proj",
    )(*args)


def _block_row(x, blk, r):
    c, w = x.shape
    x3 = x.reshape(c // blk, blk, w)
    return jnp.broadcast_to(x3[:, r:r + 1, :], x3.shape).reshape(c, w)


def _gla_direction(q, k, v, a, st, rev):
    c = q.shape[0]
    row = lax.broadcasted_iota(jnp.int32, (c, 1), 0)
    ri = lax.broadcasted_iota(jnp.int32, (c, c), 0)
    ci = lax.broadcasted_iota(jnp.int32, (c, c), 1)
    qk = jnp.sum(q * k, axis=-1, keepdims=True)
    amat = jnp.where(ri == ci, qk, 0.0)
    pq = a
    pk = jnp.ones_like(a)
    half = 1
    while half < c:
        blk = 2 * half
        in_second = (row & half) != 0
        q_side = in_second if not rev else jnp.logical_not(in_second)
        qs = jnp.where(q_side, q * pq, 0.0).astype(BF16)
        ks = jnp.where(q_side, 0.0, k * pk).astype(BF16)
        s = lax.dot_general(qs, ks, (((1,), (1,)), ((), ())), preferred_element_type=F32)
        amat = amat + jnp.where((ri // blk) == (ci // blk), s, 0.0)
        if blk >= SUBLANES:
            if not rev:
                t_near = _block_row(pq, blk, half - 1)
                t_far = _block_row(pq, blk, blk - 1)
            else:
                t_near = _block_row(pq, blk, half)
                t_far = _block_row(pq, blk, 0)
        else:
            t_near = jnp.zeros_like(pq)
            t_far = jnp.zeros_like(pq)
            pos = row & (blk - 1)
            for u in range(blk):
                if (u >= half) != rev:
                    src = (half - 1) if not rev else half
                    t_near = jnp.where(pos == u, pltpu.roll(pq, (u - src) % c, 0), t_near)
                else:
                    src = (blk - 1) if not rev else 0
                    t_far = jnp.where(pos == u, pltpu.roll(pq, (u - src) % c, 0), t_far)
        pq_new = jnp.where(q_side, pq * t_near, pq)
        pk = jnp.where(q_side, pk, pk * t_far)
        pq = pq_new
        half = blk
    o = jnp.dot(amat.astype(BF16), v, preferred_element_type=F32)
    o = o + lax.dot_general((q * pq).astype(BF16), st.astype(BF16), (((1,), (1,)), ((), ())),
                            preferred_element_type=F32)
    total = pq[c - 1:c, :] if not rev else pq[0:1, :]
    kv = lax.dot_general(v, (k * pk).astype(BF16), (((0,), (0,)), ((), ())), preferred_element_type=F32)
    return o, st * total + kv


GLA_HEADS_PER_STEP = 4


def _gla_body(qf_ref, kf_ref, vf_ref, lf_ref, qb_ref, kb_ref, vb_ref, lb_ref, up_ref, bias_ref, s0_ref,
              of_ref, ob_ref, sout_ref, st_ref):
    j = pl.program_id(2)

    @pl.when(j == 0)
    def _():
        st_ref[...] = s0_ref[0]

    scale = GLA_DK ** -0.5
    for d, (q_ref, k_ref, v_ref, l_ref, o_ref) in enumerate(((qf_ref, kf_ref, vf_ref, lf_ref, of_ref),
                                                             (qb_ref, kb_ref, vb_ref, lb_ref, ob_ref))):
        x = jnp.dot(l_ref[...], up_ref[d], precision=lax.Precision.HIGHEST, preferred_element_type=F32) + bias_ref[d]
        g = (jnp.minimum(x, 0.0) - jnp.log(1.0 + jnp.exp(-jnp.abs(x)))) * (1.0 / GLA_GATE_NORM)
        a = jnp.exp(g)
        for hh in range(GLA_HEADS_PER_STEP):
            ks = slice(hh * GLA_DK, (hh + 1) * GLA_DK)
            vs = slice(hh * GLA_DV, (hh + 1) * GLA_DV)
            q = q_ref[:, ks].astype(F32) * scale
            k = k_ref[:, ks].astype(F32)
            o, st_new = _gla_direction(q, k, v_ref[:, vs], a[:, ks], st_ref[d, hh], rev=(d == 1))
            o_ref[:, vs] = o.astype(o_ref.dtype)
            st_ref[d, hh] = st_new

    @pl.when(j == pl.num_programs(2) - 1)
    def _():
        sout_ref[0] = st_ref[...]


def _gla(q, k, v, lr, up_pad, bias, s0, batch):
    t = q.shape[0]
    seq = t // batch
    c = min(GLA_CHUNK, seq)
    nc = seq // c
    hps = GLA_HEADS_PER_STEP
    kw, vw = hps * GLA_DK, hps * GLA_DV
    fwd = lambda b, h, j: (b * nc + j, h)
    bwd = lambda b, h, j: (b * nc + nc - 1 - j, h)
    fwd0 = lambda b, h, j: (b * nc + j, 0)
    bwd0 = lambda b, h, j: (b * nc + nc - 1 - j, 0)
    st_spec = pl.BlockSpec((1, 2, hps, GLA_DV, GLA_DK), lambda b, h, j: (b, 0, h, 0, 0))
    return pl.pallas_call(
        _gla_body,
        out_shape=[jax.ShapeDtypeStruct((t, GLA_VW), BF16), jax.ShapeDtypeStruct((t, GLA_VW), BF16),
                   jax.ShapeDtypeStruct(s0.shape, F32)],
        grid=(batch, GLA_HEADS // hps, nc),
        in_specs=[pl.BlockSpec((c, kw), fwd), pl.BlockSpec((c, kw), fwd), pl.BlockSpec((c, vw), fwd),
                  pl.BlockSpec((c, LANES), fwd0),
                  pl.BlockSpec((c, kw), bwd), pl.BlockSpec((c, kw), bwd), pl.BlockSpec((c, vw), bwd),
                  pl.BlockSpec((c, LANES), bwd0),
                  pl.BlockSpec((2, LANES, kw), lambda b, h, j: (0, 0, h)),
                  pl.BlockSpec((2, 1, kw), lambda b, h, j: (0, 0, h)),
                  st_spec],
        out_specs=[pl.BlockSpec((c, vw), fwd), pl.BlockSpec((c, vw), bwd), st_spec],
        scratch_shapes=[pltpu.VMEM((2, hps, GLA_DV, GLA_DK), F32)],
        compiler_params=_cparams(("parallel", "parallel", "arbitrary")),
        name="gla_scan",
    )(q, k, v, lr, q, k, v, lr, up_pad, bias, s0)


FFT_P = 128


def _fft_tables(seq):
    q = seq // FFT_P
    n_hi = np.arange(q)
    ang1 = 2.0 * np.pi * ((n_hi[:, None] * n_hi[None, :]) % q) / q
    w1 = np.concatenate([np.cos(ang1), -np.sin(ang1)], axis=0)
    n_lo = np.arange(FFT_P)
    k_a = np.arange(FFT_P)
    k_b = np.arange(q)
    k_full = k_b[:, None, None] + q * k_a[None, :, None]
    ang2 = 2.0 * np.pi * ((k_full * n_lo[None, None, :]) % seq) / seq
    c2, s2 = np.cos(ang2), np.sin(ang2)
    w2 = np.concatenate([np.concatenate([c2, s2], axis=2), np.concatenate([-s2, c2], axis=2)], axis=1)
    ch = np.arange(FNET_GROUP_W)
    angc = 2.0 * np.pi * ((ch[:, None] * ch[None, :]) % FNET_GROUP_W) / FNET_GROUP_W
    scale = 1.0 / np.sqrt(float(seq) * FNET_GROUP_W)
    eye = np.eye(FNET_GROUPS)
    wc = np.concatenate([np.kron(eye, np.cos(angc)), np.kron(eye, np.sin(angc))], axis=0) * scale
    return (jnp.asarray(w1, BF16), jnp.asarray(w2, BF16), jnp.asarray(wc, BF16))


def _fft1_body(z_ref, w_ref, o_ref):
    o_ref[0] = jnp.dot(w_ref[...], z_ref[0], preferred_element_type=F32).astype(o_ref.dtype)


FFT_KB = 4


def _fft2_body(a_ref, w_ref, wc_ref, o_ref):
    for i in range(a_ref.shape[2]):
        a = a_ref[0, :, i].reshape(2 * FFT_P, FNET_W)
        z = jnp.dot(w_ref[i], a, preferred_element_type=F32).astype(BF16)
        zz = jnp.concatenate([z[:FFT_P], z[FFT_P:]], axis=-1)
        o_ref[0, :, i * FNET_W:(i + 1) * FNET_W] = jnp.dot(zz, wc_ref[...],
                                                           preferred_element_type=F32).astype(o_ref.dtype)


def _fourier_mix(f, batch):
    t = f.shape[0]
    seq = t // batch
    q = seq // FFT_P
    w1, w2, wc = _fft_tables(seq)
    cols = FFT_P * FNET_W
    tn = min(8192, cols)
    z = f.reshape(batch, q, cols)
    a = pl.pallas_call(
        _fft1_body,
        out_shape=jax.ShapeDtypeStruct((batch, 2 * q, cols), BF16),
        grid=(batch, cols // tn),
        in_specs=[pl.BlockSpec((1, q, tn), lambda b, j: (b, 0, j)),
                  pl.BlockSpec((2 * q, q), lambda b, j: (0, 0))],
        out_specs=pl.BlockSpec((1, 2 * q, tn), lambda b, j: (b, 0, j)),
        compiler_params=_cparams(("parallel", "parallel")),
        name="fft_stage1",
    )(z, w1)
    a = a.reshape(batch, 2, q, FFT_P, FNET_W)
    kbs = min(FFT_KB, q)
    out = pl.pallas_call(
        _fft2_body,
        out_shape=jax.ShapeDtypeStruct((batch, FFT_P, q * FNET_W), BF16),
        grid=(batch, q // kbs),
        in_specs=[pl.BlockSpec((1, 2, kbs, FFT_P, FNET_W), lambda b, kb: (b, 0, kb, 0, 0)),
                  pl.BlockSpec((kbs, 2 * FFT_P, 2 * FFT_P), lambda b, kb: (kb, 0, 0)),
                  pl.BlockSpec((2 * FNET_W, FNET_W), lambda b, kb: (0, 0))],
        out_specs=pl.BlockSpec((1, FFT_P, kbs * FNET_W), lambda b, kb: (b, 0, kb)),
        compiler_params=_cparams(("parallel", "parallel")),
        name="fft_stage2",
    )(a, w2, wc)
    return out.reshape(batch, FFT_P, q, FNET_W).reshape(t, FNET_W)


def _rms(y, g):
    return y * lax.rsqrt(jnp.mean(y * y, axis=-1, keepdims=True) + EPS) * g


def _even_out_body(of_ref, ob_ref, r_ref, fo_ref, x_ref, mod_ref, gn_ref, w_ref, gpost_ref, o_ref):
    parts = []
    for h in range(GLA_HEADS):
        sl = slice(h * GLA_DV, (h + 1) * GLA_DV)
        o = of_ref[:, sl].astype(F32) + ob_ref[:, sl].astype(F32)
        r = r_ref[:, sl].astype(F32)
        parts.append((_rms(o, gn_ref[...]) * (r * jax.nn.sigmoid(r))).astype(BF16))
    y = jnp.dot(jnp.concatenate(parts, axis=-1), w_ref[:GLA_VW, :], preferred_element_type=F32)
    y = y + jnp.dot(fo_ref[...], w_ref[GLA_VW:, :], preferred_element_type=F32)
    o_ref[...] = x_ref[...] + mod_ref[0, 2:3, :] * _rms(y, gpost_ref[...])


def _even_out(o_f, o_b, r, fo, x2, mod, gnorm, w_out, g_post, batch):
    t, d = x2.shape
    tm = min(ROW_TILE, t)
    tiles_per_mod = (t // batch) // tm
    row = lambda w: pl.BlockSpec((tm, w), lambda i: (i, 0))
    full = lambda a: pl.BlockSpec(a.shape, lambda i: (0,) * a.ndim)
    gn = gnorm.reshape(1, GLA_DV)
    gp = g_post.reshape(1, d)
    return pl.pallas_call(
        _even_out_body,
        out_shape=jax.ShapeDtypeStruct((t, d), F32),
        grid=(t // tm,),
        in_specs=[row(GLA_VW), row(GLA_VW), row(GLA_VW), row(FNET_W), row(d),
                  pl.BlockSpec((1, 8, d), lambda i: (i // tiles_per_mod, 0, 0)),
                  full(gn), full(w_out), full(gp)],
        out_specs=row(d),
        compiler_params=_cparams(("parallel",)),
        name="even_out",
    )(o_f, o_b, r, fo, x2, mod, gn, w_out, gp)


MOE_TILE = 512
MOE_BLOCK = 512
NEG_BIG = -1e30


def _route_body(x_ref, mod_ref, g_ref, rw_ref, rb_ref, h_ref, lpos_ref, gate_ref, cnt_ref, cbase_ref, base_ref):
    i = pl.program_id(0)

    @pl.when(i == 0)
    def _():
        base_ref[...] = jnp.zeros_like(base_ref)

    tm = x_ref.shape[0]
    h = _modnorm(x_ref[...], g_ref[...], mod_ref, 3, 4)
    h_hi = h.astype(BF16)
    h_ref[...] = h_hi
    h_lo = (h - h_hi.astype(F32)).astype(BF16)
    both = jnp.dot(h_hi, rw_ref[...], preferred_element_type=F32)
    logits = (both[:, :LANES] + both[:, LANES:]
              + jnp.dot(h_lo, rw_ref[:, :LANES], preferred_element_type=F32) + rb_ref[...])
    lane = lax.broadcasted_iota(jnp.int32, (tm, LANES), 1).astype(F32)
    work = logits
    vals, hots = [], []
    for _k in range(TOP_K):
        m = jnp.max(work, axis=-1, keepdims=True)
        idx = jnp.min(jnp.where(work == m, lane, float(LANES)), axis=-1, keepdims=True)
        hot = lane == idx
        vals.append(m)
        hots.append(hot)
        work = jnp.where(hot, -jnp.inf, work)
    es = [jnp.exp(v - vals[0]) for v in vals]
    inv = 1.0 / (es[0] + es[1] + es[2] + es[3])
    sel = sum(hh.astype(F32) for hh in hots)
    ri = lax.broadcasted_iota(jnp.int32, (tm, tm), 0)
    ci = lax.broadcasted_iota(jnp.int32, (tm, tm), 1)
    earlier = jnp.where(ci < ri, 1.0, 0.0).astype(BF16)
    cum = jnp.dot(earlier, sel.astype(BF16), preferred_element_type=F32)
    cnt = jnp.sum(sel, axis=0, keepdims=True)
    ei = lax.broadcasted_iota(jnp.int32, (LANES, LANES), 0)
    ej = lax.broadcasted_iota(jnp.int32, (LANES, LANES), 1)
    before = jnp.where(ei < ej, 1.0, 0.0)
    loff = jnp.dot(jnp.broadcast_to(cnt, (SUBLANES, LANES)), before, precision=lax.Precision.HIGHEST,
                   preferred_element_type=F32)[0:1]
    where_to = cum + loff
    for k in range(TOP_K):
        p = jnp.sum(jnp.where(hots[k], where_to, 0.0), axis=-1, keepdims=True)
        lpos_ref[:, k:k + 1] = p.astype(jnp.int32)
        gate_ref[:, k:k + 1] = es[k] * inv
    cnt_ref[0] = cnt.astype(jnp.int32)
    cbase_ref[0] = base_ref[...].astype(jnp.int32)
    base_ref[...] = base_ref[...] + cnt


def _route(x2, mod, g_pre, rw, rb, batch):
    t, d = x2.shape
    tm = min(MOE_TILE, t)
    nt = t // tm
    tiles_per_mod = (t // batch) // tm
    rw_pad = jnp.zeros((d, LANES), F32).at[:, :N_EXPERTS].set(rw)
    rw_hi = rw_pad.astype(BF16)
    rw_pad = jnp.concatenate([rw_hi, (rw_pad - rw_hi.astype(F32)).astype(BF16)], axis=1)
    rb_pad = jnp.full((1, LANES), NEG_BIG, F32).at[0, :N_EXPERTS].set(rb)
    return pl.pallas_call(
        _route_body,
        out_shape=[jax.ShapeDtypeStruct((t, d), BF16), jax.ShapeDtypeStruct((t, TOP_K), jnp.int32),
                   jax.ShapeDtypeStruct((t, TOP_K), F32), jax.ShapeDtypeStruct((nt, 1, LANES), jnp.int32),
                   jax.ShapeDtypeStruct((nt, 1, LANES), jnp.int32)],
        grid=(nt,),
        in_specs=[pl.BlockSpec((tm, d), lambda i: (i, 0)),
                  pl.BlockSpec((1, 8, d), lambda i: (i // tiles_per_mod, 0, 0)),
                  pl.BlockSpec((1, d), lambda i: (0, 0)),
                  pl.BlockSpec((d, 2 * LANES), lambda i: (0, 0)),
                  pl.BlockSpec((1, LANES), lambda i: (0, 0))],
        out_specs=[pl.BlockSpec((tm, d), lambda i: (i, 0)), pl.BlockSpec((tm, TOP_K), lambda i: (i, 0)),
                   pl.BlockSpec((tm, TOP_K), lambda i: (i, 0)), pl.BlockSpec((1, 1, LANES), lambda i: (i, 0, 0)),
                   pl.BlockSpec((1, 1, LANES), lambda i: (i, 0, 0))],
        scratch_shapes=[pltpu.VMEM((1, LANES), F32)],
        compiler_params=_cparams(("arbitrary",)),
        name="moe_route",
    )(x2, mod, g_pre.reshape(1, d), rw_pad, rb_pad)


def _moe_plan(cnt, cbase, n_slots):
    cnt = cnt[:, 0, :N_EXPERTS]
    cbase = cbase[:, 0, :N_EXPERTS]
    total = cbase[-1] + cnt[-1]
    padded = (total + MOE_BLOCK - 1) // MOE_BLOCK * MOE_BLOCK
    pad_end = jnp.cumsum(padded)
    start = pad_end - padded
    n_blocks = (n_slots + N_EXPERTS * (MOE_BLOCK - 1) + MOE_BLOCK - 1) // MOE_BLOCK
    n_used = pad_end[-1] // MOE_BLOCK
    blk = jnp.arange(n_blocks, dtype=jnp.int32)
    first_row = jnp.minimum(blk, n_used - 1) * MOE_BLOCK
    block_e = jnp.minimum(jnp.sum((pad_end[None, :] <= first_row[:, None]).astype(jnp.int32), axis=1), N_EXPERTS - 1)
    loff = jnp.cumsum(cnt, axis=1) - cnt
    gdst = start[None, :] + cbase
    flat = lambda a: a.reshape(-1).astype(jnp.int32)
    fill = (flat(padded - total), jnp.zeros((N_EXPERTS,), jnp.int32), flat(start + total))
    return (n_blocks, block_e, n_used.reshape(1).astype(jnp.int32), flat(cnt), flat(loff), flat(gdst)) + fill


def _strip_copies(cnt_s, loff_s, gdst_s, tile, local_ref, global_ref, sem, to_global, max_rows):
    def per_expert(e, carry):
        n = cnt_s[tile * N_EXPERTS + e]
        src0 = loff_s[tile * N_EXPERTS + e]
        dst0 = gdst_s[tile * N_EXPERTS + e]
        off = jnp.int32(0)
        size = max_rows
        while size >= 1:
            take = n & size

            @pl.when(take != 0)
            def _(off=off, size=size):
                loc = local_ref.at[pl.ds(src0 + off, size)]
                glo = global_ref.at[pl.ds(dst0 + off, size)]
                if to_global:
                    pltpu.make_async_copy(loc, glo, sem).start()
                else:
                    pltpu.make_async_copy(glo, loc, sem).start()

            off = off + take
            size //= 2
        return carry

    lax.fori_loop(0, N_EXPERTS, per_expert, 0)


def _dispatch_body(cnt_s, loff_s, gdst_s, zcnt_s, zoff_s, zdst_s, nu_s, h_ref, lpos_ref, xb_ref, xs0_ref, xs1_ref, zero_ref,
                   sem0, sem1, zsem, *, n_blocks, n_fill, n_tiles):
    i = pl.program_id(0)
    tm = h_ref.shape[0]
    rows = TOP_K * tm
    lane = lax.broadcasted_iota(jnp.int32, (tm, rows), 1)
    onehot = jnp.zeros((tm, rows), F32)
    for k in range(TOP_K):
        onehot = onehot + jnp.where(lane == lpos_ref[:, k:k + 1], 1.0, 0.0)
    xs = lax.dot_general(onehot.astype(BF16), h_ref[...], (((0,), (0,)), ((), ())), preferred_element_type=F32)
    slots = ((xs0_ref, sem0), (xs1_ref, sem1))

    def drain(slot):
        xs_ref, sem = slots[slot]
        pltpu.make_async_copy(xs_ref, xb_ref.at[pl.ds(0, rows)], sem).wait()

    for slot in range(2):
        @pl.when(i % 2 == slot)
        def _(slot=slot):
            @pl.when(i >= 2)
            def _():
                drain(slot)

            xs_ref, sem = slots[slot]
            xs_ref[...] = xs.reshape(rows, 1, xs.shape[-1])
            _strip_copies(cnt_s, loff_s, gdst_s, i, xs_ref, xb_ref, sem, True, tm)

    @pl.when(i == n_tiles - 1)
    def _():
        drain((n_tiles - 1) % 2)
        if n_tiles >= 2:
            drain(n_tiles % 2)
        zero_ref[...] = jnp.zeros_like(zero_ref)
        _strip_copies(zcnt_s, zoff_s, zdst_s, 0, zero_ref, xb_ref, zsem, True, MOE_BLOCK // 2)

        def per_block(b, carry):
            @pl.when(b >= nu_s[0])
            def _():
                pltpu.make_async_copy(zero_ref, xb_ref.at[pl.ds(b * MOE_BLOCK, MOE_BLOCK)], zsem).start()
            return carry

        lax.fori_loop(0, n_blocks, per_block, 0)
        pltpu.make_async_copy(xb_ref.at[pl.ds(0, n_fill)], xb_ref.at[pl.ds(0, n_fill)], zsem).wait()


def _dispatch(h, lpos, plan):
    n_blocks, _, n_used, cnt_s, loff_s, gdst_s, zcnt_s, zoff_s, zdst_s = plan
    t, d = h.shape
    tm = min(MOE_TILE, t)
    rows = TOP_K * tm
    n_fill = n_blocks * MOE_BLOCK - TOP_K * t
    return pl.pallas_call(
        functools.partial(_dispatch_body, n_blocks=n_blocks, n_fill=n_fill, n_tiles=t // tm),
        out_shape=jax.ShapeDtypeStruct((n_blocks * MOE_BLOCK, 1, d), F32),
        grid_spec=pltpu.PrefetchScalarGridSpec(
            num_scalar_prefetch=7,
            grid=(t // tm,),
            in_specs=[pl.BlockSpec((tm, d), lambda i, *_: (i, 0)),
                      pl.BlockSpec((tm, TOP_K), lambda i, *_: (i, 0))],
            out_specs=pl.BlockSpec(memory_space=pl.ANY),
            scratch_shapes=[pltpu.VMEM((rows, 1, d), F32), pltpu.VMEM((rows, 1, d), F32),
                            pltpu.VMEM((MOE_BLOCK, 1, d), F32),
                            pltpu.SemaphoreType.DMA, pltpu.SemaphoreType.DMA, pltpu.SemaphoreType.DMA],
        ),
        compiler_params=_cparams(("arbitrary",)),
        name="moe_dispatch",
    )(cnt_s, loff_s, gdst_s, zcnt_s, zoff_s, zdst_s, n_used, h, lpos)


def _experts_body(be_s, nu_s, x_ref, wgu_ref, bgu_ref, wd_ref, bd_ref, o_ref, x2_ref, wgu_bf, wd_bf):
    i = pl.program_id(0)
    rows, _, d = x_ref.shape

    @pl.when(i < nu_s[0])
    def _():
        @pl.when((i == 0) | (be_s[i] != be_s[jnp.maximum(i - 1, 0)]))
        def _():
            wgu_bf[...] = wgu_ref[0].astype(BF16)
            wd_bf[...] = wd_ref[0].astype(BF16)

        x2_ref[...] = x_ref[...].reshape(rows, d)
        x = x2_ref[...].astype(BF16)
        gu = jnp.dot(x, wgu_bf[...], preferred_element_type=F32) + bgu_ref[0]
        g = jnp.minimum(gu[:, :D_FF], SWIGLU_LIMIT)
        u = jnp.clip(gu[:, D_FF:], -SWIGLU_LIMIT, SWIGLU_LIMIT)
        act = (u + 1.0) * (g * jax.nn.sigmoid(SWIGLU_ALPHA * g))
        y = jnp.dot(act.astype(BF16), wd_bf[...], preferred_element_type=F32) + bd_ref[0]
        o_ref[...] = y.reshape(rows, 1, d)

    @pl.when(i >= nu_s[0])
    def _():
        o_ref[...] = jnp.zeros_like(o_ref)


def _experts(xb, plan, layer, w_gu, b_gu, w_down, b_down):
    n_blocks, block_e, n_used = plan[:3]
    d = xb.shape[-1]
    base = layer * N_EXPERTS
    flat = lambda a: a.reshape((-1,) + a.shape[2:])
    return pl.pallas_call(
        _experts_body,
        out_shape=jax.ShapeDtypeStruct(xb.shape, F32),
        grid_spec=pltpu.PrefetchScalarGridSpec(
            num_scalar_prefetch=2,
            grid=(n_blocks,),
            in_specs=[pl.BlockSpec((MOE_BLOCK, 1, d), lambda i, be, nu: (i, 0, 0)),
                      pl.BlockSpec((1, d, 2 * D_FF), lambda i, be, nu: (base + be[i], 0, 0)),
                      pl.BlockSpec((1, 1, 2 * D_FF), lambda i, be, nu: (base + be[i], 0, 0)),
                      pl.BlockSpec((1, D_FF, d), lambda i, be, nu: (base + be[i], 0, 0)),
                      pl.BlockSpec((1, 1, d), lambda i, be, nu: (base + be[i], 0, 0))],
            out_specs=pl.BlockSpec((MOE_BLOCK, 1, d), lambda i, be, nu: (i, 0, 0)),
            scratch_shapes=[pltpu.VMEM((MOE_BLOCK, d), F32), pltpu.VMEM((d, 2 * D_FF), BF16),
                            pltpu.VMEM((D_FF, d), BF16)],
        ),
        compiler_params=_cparams(("arbitrary",)),
        name="moe_experts",
    )(block_e, n_used, xb, flat(w_gu), flat(b_gu)[:, None, :], flat(w_down), flat(b_down)[:, None, :])


def _combine_body(cnt_s, loff_s, gdst_s, yb_ref, lpos_ref, gate_ref, x_ref, mod_ref, g_ref, o_ref, ys0_ref, ys1_ref,
                  ys2_ref, sem0, sem1, *, n_tiles):
    i = pl.program_id(0)
    tm = x_ref.shape[0]
    rows = TOP_K * tm
    slots = ((ys0_ref, sem0), (ys1_ref, sem1))

    def fetch(tile, slot):
        ys_ref, sem = slots[slot]
        _strip_copies(cnt_s, loff_s, gdst_s, tile, ys_ref, yb_ref, sem, False, tm)

    @pl.when(i == 0)
    def _():
        fetch(0, 0)

    for slot in range(2):
        @pl.when(i % 2 == slot)
        def _(slot=slot):
            @pl.when(i + 1 < n_tiles)
            def _():
                fetch(i + 1, 1 - slot)

            ys_ref, sem = slots[slot]
            pltpu.make_async_copy(yb_ref.at[pl.ds(0, rows)], ys_ref, sem).wait()
            ys2_ref[...] = ys_ref[...].reshape(rows, ys_ref.shape[-1])

    lane = lax.broadcasted_iota(jnp.int32, (tm, rows), 1)
    weights = jnp.zeros((tm, rows), F32)
    for k in range(TOP_K):
        weights = weights + jnp.where(lane == lpos_ref[:, k:k + 1], gate_ref[:, k:k + 1], 0.0)
    ys = ys2_ref[...].astype(BF16)
    f = jnp.dot(weights.astype(BF16), ys, preferred_element_type=F32)
    o_ref[...] = x_ref[...] + mod_ref[0, 5:6, :] * _rms(f, g_ref[...])


def _combine(yb, lpos, gate, x2, mod, g_post, plan, batch):
    cnt_s, loff_s, gdst_s = plan[3:6]
    t, d = x2.shape
    tm = min(MOE_TILE, t)
    rows = TOP_K * tm
    tiles_per_mod = (t // batch) // tm
    return pl.pallas_call(
        functools.partial(_combine_body, n_tiles=t // tm),
        out_shape=jax.ShapeDtypeStruct((t, d), F32),
        grid_spec=pltpu.PrefetchScalarGridSpec(
            num_scalar_prefetch=3,
            grid=(t // tm,),
            in_specs=[pl.BlockSpec(memory_space=pl.ANY),
                      pl.BlockSpec((tm, TOP_K), lambda i, *_: (i, 0)),
                      pl.BlockSpec((tm, TOP_K), lambda i, *_: (i, 0)),
                      pl.BlockSpec((tm, d), lambda i, *_: (i, 0)),
                      pl.BlockSpec((1, 8, d), lambda i, *_: (i // tiles_per_mod, 0, 0)),
                      pl.BlockSpec((1, d), lambda i, *_: (0, 0))],
            out_specs=pl.BlockSpec((tm, d), lambda i, *_: (i, 0)),
            scratch_shapes=[pltpu.VMEM((rows, 1, d), F32), pltpu.VMEM((rows, 1, d), F32), pltpu.VMEM((rows, d), F32),
                            pltpu.SemaphoreType.DMA, pltpu.SemaphoreType.DMA],
        ),
        compiler_params=_cparams(("arbitrary",)),
        name="moe_combine",
    )(cnt_s, loff_s, gdst_s, yb, lpos, gate, x2, mod, g_post.reshape(1, d))


def _moe_ffn(x2, mod, g_pre, g_post, rw, rb, layer, w_gu, b_gu, w_down, b_down, batch):
    h, lpos, gate, cnt, cbase = _route(x2, mod, g_pre, rw, rb, batch)
    plan = _moe_plan(cnt, cbase, x2.shape[0] * TOP_K)
    xb = _dispatch(h, lpos, plan)
    yb = _experts(xb, plan, layer, w_gu, b_gu, w_down, b_down)
    return _combine(yb, lpos, gate, x2, mod, g_post, plan, batch)


CONF_HALO = 16


def _odd_body(a_ref, ap_ref, an_ref, gb_ref, cu_ref, cup_ref, cun_ref, x_ref, mod_ref, dw_ref, dwb_ref, lng_ref,
              lnb_ref, sc_ref, w_ref, gpost_ref, o_ref, shift_ref, *, tiles_per_seq):
    i = pl.program_id(0)
    tm = x_ref.shape[0]
    first = (i % tiles_per_seq) == 0
    last = (i % tiles_per_seq) == tiles_per_seq - 1

    def glu(ref):
        v = ref[...].astype(F32)
        return v[:, :CONF_W] * jax.nn.sigmoid(v[:, CONF_W:])

    pad = CONF_KERNEL // 2
    ext = jnp.concatenate([jnp.where(first, 0.0, glu(ap_ref)), glu(a_ref), jnp.where(last, 0.0, glu(an_ref))], axis=0)
    acc = jnp.zeros((tm, CONF_W), F32) + dwb_ref[...]
    span = shift_ref.shape[1]
    for r in range(SUBLANES):
        shift_ref[r] = ext[r:r + span, :]
    for k in range(CONF_KERNEL):
        lo = CONF_HALO - pad + k
        base = (lo // SUBLANES) * SUBLANES
        acc = acc + dw_ref[k:k + 1, :] * shift_ref[lo % SUBLANES, base:base + tm, :]
    mu = jnp.mean(acc, axis=-1, keepdims=True)
    xc = acc - mu
    hn = xc * lax.rsqrt(jnp.mean(xc * xc, axis=-1, keepdims=True) + EPS) * lng_ref[...] + lnb_ref[...]
    hc = (hn * jax.nn.sigmoid(hn)).astype(BF16)

    def gated(ref):
        v = ref[...].astype(F32)
        return v[:, :SCONV_W] * v[:, SCONV_W:]

    zc = gated(cu_ref)
    zext = jnp.concatenate([jnp.where(first, 0.0, gated(cup_ref)), zc, jnp.where(last, 0.0, gated(cun_ref))], axis=0)
    z = sc_ref[0:1, :] * zext[0:tm] + sc_ref[1:2, :] * zc + sc_ref[2:3, :] * zext[2 * GRID_W:2 * GRID_W + tm]
    z = (gb_ref[...].astype(F32) * z).astype(BF16)
    y = jnp.dot(hc, w_ref[:CONF_W, :], preferred_element_type=F32) + jnp.dot(z, w_ref[CONF_W:, :],
                                                                            preferred_element_type=F32)
    o_ref[...] = x_ref[...] + mod_ref[0, 2:3, :] * _rms(y, gpost_ref[...])


def _odd_mix(a, gb, cu, x2, mod, conf_dw, conf_dw_b, ln_g, ln_b, sconv, w_out, g_post, batch):
    t, d = x2.shape
    seq = t // batch
    tm = min(ROW_TILE, seq)
    tiles_per_seq = seq // tm
    nt = t // tm
    hb_c, hb_s = tm // CONF_HALO, tm // GRID_W
    row = lambda w: pl.BlockSpec((tm, w), lambda i: (i, 0))
    prev = lambda rows, per, w: pl.BlockSpec((rows, w), lambda i: (jnp.maximum(i * per - 1, 0), 0))
    nxt = lambda rows, per, w: pl.BlockSpec((rows, w), lambda i: (jnp.minimum((i + 1) * per, nt * per - 1), 0))
    full = lambda arr: pl.BlockSpec(arr.shape, lambda i: (0,) * arr.ndim)
    smalls = [conf_dw, conf_dw_b.reshape(1, -1), ln_g.reshape(1, -1), ln_b.reshape(1, -1), sconv, w_out,
              g_post.reshape(1, d)]
    return pl.pallas_call(
        functools.partial(_odd_body, tiles_per_seq=tiles_per_seq),
        out_shape=jax.ShapeDtypeStruct((t, d), F32),
        grid=(nt,),
        in_specs=[row(2 * CONF_W), prev(CONF_HALO, hb_c, 2 * CONF_W), nxt(CONF_HALO, hb_c, 2 * CONF_W),
                  row(SCONV_W),
                  row(2 * SCONV_W), prev(GRID_W, hb_s, 2 * SCONV_W), nxt(GRID_W, hb_s, 2 * SCONV_W),
                  row(d), pl.BlockSpec((1, 8, d), lambda i: (i // tiles_per_seq, 0, 0))] + [full(s) for s in smalls],
        out_specs=row(d),
        scratch_shapes=[pltpu.VMEM((SUBLANES, tm + CONF_HALO + SUBLANES, CONF_W), F32)],
        compiler_params=_cparams(("parallel",)),
        name="odd_mix",
    )(a, a, a, gb, cu, cu, cu, x2, mod, *smalls)


COND_ROWS = 16


def _mod_rows(mods_layer):
    m = mods_layer.reshape(COND_ROWS, 6, D_MODEL)
    return jnp.concatenate([m, jnp.zeros((COND_ROWS, 2, D_MODEL), F32)], axis=1)


def _even_weights(w_in, gk_up, gk_b):
    cuts = np.cumsum((GLA_KW, GLA_KW, GLA_VW, GLA_VW, GLA_LOWRANK, GLA_LOWRANK, FNET_W))[:-1]
    wq, wk, wv, wr, wlf, wlb, wf = jnp.split(w_in, [int(v) for v in cuts], axis=-1)
    w_main = jnp.concatenate([wq, wk, wv, wr, wf], axis=-1).astype(BF16)
    pad = jnp.zeros((w_in.shape[0], LANES - 2 * GLA_LOWRANK), w_in.dtype)
    w_tail = jnp.concatenate([wlf, wlb, pad], axis=-1).astype(BF16)
    up_pad = jnp.zeros((2, LANES, GLA_KW), F32)
    up_pad = up_pad.at[0, :GLA_LOWRANK].set(gk_up[0]).at[1, GLA_LOWRANK:2 * GLA_LOWRANK].set(gk_up[1])
    return w_main, w_tail, up_pad, gk_b.reshape(2, 1, GLA_KW)


EVEN_WIDTHS = (GLA_KW, GLA_KW, GLA_VW, GLA_VW, FNET_W)


def _even_gla(x2, ctx2, mod, g_pre, w_main, w_tail, up_pad, bias, batch):
    nb = batch
    cq, ck, cv, _, _, clr = _norm_proj(ctx2, mod[nb:nb + 1], ctx2.shape[0], g_pre, w_main, EVEN_WIDTHS, w_tail)
    zero = jnp.zeros((nb, 2, GLA_HEADS, GLA_DV, GLA_DK), F32)
    _, _, s_ctx = _gla(cq, ck, cv, clr, up_pad, bias, zero, nb)
    q, k, v, r, f, lr = _norm_proj(x2, mod[:nb], x2.shape[0] // nb, g_pre, w_main, EVEN_WIDTHS, w_tail)
    o_f, o_b, _ = _gla(q, k, v, lr, up_pad, bias, s_ctx, nb)
    return r, f, o_f, o_b


def kernel(x, c, ctx, c_ctx, mod_w, mod_b, norm_mix_pre, norm_mix_post, norm_ffn_pre, norm_ffn_post, ev_w_in, ev_gk_up, ev_gk_b, ev_gnorm, ev_w_out, od_w_in, od_conf_dw, od_conf_dw_b, od_conf_ln_g, od_conf_ln_b, od_sconv, od_w_out, router_w, router_b, exp_w_gu, exp_b_gu, exp_w_down, exp_b_down):
    nb, seq, d = x.shape
    cond = jnp.concatenate([c, c_ctx[None], jnp.zeros((COND_ROWS - nb - 1, d), F32)], axis=0)
    mods = _adaln(cond, mod_w, mod_b)
    x2 = x.reshape(nb * seq, d)
    ctx2 = ctx.reshape(nb * ctx.shape[1], d)
    depth = mod_w.shape[0]
    assert depth == 2, "layer pattern implemented for one even layer followed by one odd layer"
    mod = _mod_rows(mods[0])
    wm, wt, up, bias = _even_weights(ev_w_in[0], ev_gk_up[0], ev_gk_b[0])
    r, f, o_f, o_b = _even_gla(x2, ctx2, mod, norm_mix_pre[0], wm, wt, up, bias, nb)
    fo = _fourier_mix(f, nb)
    x2 = _even_out(o_f, o_b, r, fo, x2, mod, ev_gnorm[0], ev_w_out[0].astype(BF16), norm_mix_post[0], nb)
    x2 = _moe_ffn(x2, mod, norm_ffn_pre[0], norm_ffn_post[0], router_w[0], router_b[0],
                  0, exp_w_gu, exp_b_gu, exp_w_down, exp_b_down, nb)
    mod = _mod_rows(mods[1])
    a, gb, cu = _norm_proj(x2, mod[:nb], seq, norm_mix_pre[1], od_w_in[0].astype(BF16),
                           (2 * CONF_W, SCONV_W, 2 * SCONV_W))
    x2 = _odd_mix(a, gb, cu, x2, mod, od_conf_dw[0], od_conf_dw_b[0], od_conf_ln_g[0], od_conf_ln_b[0], od_sconv[0],
                  od_w_out[0].astype(BF16), norm_mix_post[1], nb)
    x2 = _moe_ffn(x2, mod, norm_ffn_pre[1], norm_ffn_post[1], router_w[1], router_b[1],
                  1, exp_w_gu, exp_b_gu, exp_w_down, exp_b_down, nb)
    return x2.reshape(nb, seq, d)
```

```python
import functools

import numpy as np
import jax
import jax.numpy as jnp
from jax import lax
from jax.experimental import pallas as pl
from jax.experimental.pallas import tpu as pltpu

F32 = jnp.float32
BF16 = jnp.bfloat16

D_MODEL = 1024
EPS = 1e-6
GRID_W = 64
GLA_HEADS = 4
GLA_DK = 128
GLA_DV = 256
GLA_LOWRANK = 16
GLA_GATE_NORM = 16.0
GLA_KW = GLA_HEADS * GLA_DK
GLA_VW = GLA_HEADS * GLA_DV
FNET_GROUPS = 4
FNET_GROUP_W = 128
FNET_W = FNET_GROUPS * FNET_GROUP_W
CONF_W = 512
CONF_KERNEL = 31
SCONV_W = 512
N_EXPERTS = 32
TOP_K = 4
D_FF = D_MODEL
SWIGLU_LIMIT = 7.0
SWIGLU_ALPHA = 1.702

LANES = 128
SUBLANES = 8
VMEM_LIMIT = 56 * 1024 * 1024

GLA_CHUNK = 128
ROW_TILE = 512


def _cparams(sem):
    return pltpu.CompilerParams(dimension_semantics=sem, vmem_limit_bytes=VMEM_LIMIT)


def _adaln_body(c_ref, w_ref, b_ref, o_ref):
    c = c_ref[...]
    s = c * jax.nn.sigmoid(c)
    o_ref[0] = jnp.dot(s, w_ref[0], precision=lax.Precision.HIGHEST, preferred_element_type=F32) + b_ref[0]


def _adaln(cond, mod_w, mod_b):
    depth, d, n = mod_w.shape
    rows = cond.shape[0]
    tn = 1536
    return pl.pallas_call(
        _adaln_body,
        out_shape=jax.ShapeDtypeStruct((depth, rows, n), F32),
        grid=(depth, n // tn),
        in_specs=[pl.BlockSpec((rows, d), lambda l, j: (0, 0)),
                  pl.BlockSpec((1, d, tn), lambda l, j: (l, 0, j)),
                  pl.BlockSpec((1, 1, tn), lambda l, j: (l, 0, j))],
        out_specs=pl.BlockSpec((1, rows, tn), lambda l, j: (l, 0, j)),
        compiler_params=_cparams(("arbitrary", "arbitrary")),
        name="adaln",
    )(cond, mod_w, mod_b.reshape(depth, 1, n))


def _modnorm(x, g, mod_ref, shift_row, scale_row):
    ms = jnp.mean(x * x, axis=-1, keepdims=True)
    y = x * lax.rsqrt(ms + EPS) * g
    return y * (1.0 + mod_ref[0, scale_row:scale_row + 1, :]) + mod_ref[0, shift_row:shift_row + 1, :]


def _normproj_body(x_ref, mod_ref, g_ref, w_ref, *rest, widths, has_f32_tail):
    if has_f32_tail:
        wt_ref, outs = rest[0], rest[1:]
    else:
        wt_ref, outs = None, rest
    h = _modnorm(x_ref[...], g_ref[...], mod_ref, 0, 1).astype(BF16)
    col = 0
    for o_ref, wd in zip(outs, widths):
        step = 512
        for j in range(0, wd, step):
            o_ref[:, j:j + step] = jnp.dot(h, w_ref[:, col + j:col + j + step],
                                           preferred_element_type=F32).astype(o_ref.dtype)
        col += wd
    if has_f32_tail:
        outs[-1][...] = jnp.dot(h, wt_ref[...], preferred_element_type=F32)


def _norm_proj(x2, mod, rows_per_mod, g, w, widths, w_tail=None):
    t, d = x2.shape
    tm = min(ROW_TILE, t)
    n = w.shape[1]
    assert sum(widths) == n and all(wd % 512 == 0 for wd in widths) and rows_per_mod % tm == 0
    tiles_per_mod = rows_per_mod // tm
    in_specs = [pl.BlockSpec((tm, d), lambda i: (i, 0)),
                pl.BlockSpec((1, 8, d), lambda i: (i // tiles_per_mod, 0, 0)),
                pl.BlockSpec((1, d), lambda i: (0, 0)),
                pl.BlockSpec((d, n), lambda i: (0, 0))]
    args = [x2, mod, g.reshape(1, d), w]
    out_shape = [jax.ShapeDtypeStruct((t, wd), BF16) for wd in widths]
    out_specs = [pl.BlockSpec((tm, wd), lambda i: (i, 0)) for wd in widths]
    if w_tail is not None:
        in_specs.append(pl.BlockSpec((d, LANES), lambda i: (0, 0)))
        args.append(w_tail)
        out_shape.append(jax.ShapeDtypeStruct((t, LANES), F32))
        out_specs.append(pl.BlockSpec((tm, LANES), lambda i: (i, 0)))
    return pl.pallas_call(
        functools.partial(_normproj_body, widths=tuple(widths), has_f32_tail=w_tail is not None),
        out_shape=out_shape,
        grid=(t // tm,),
        in_specs=in_specs,
        out_specs=out_specs,
        compiler_params=_cparams(("parallel",)),
        name="norm_proj",
    )(*args)


def _block_row(x, blk, r):
    c, w = x.shape
    x3 = x.reshape(c // blk, blk, w)
    return jnp.broadcast_to(x3[:, r:r + 1, :], x3.shape).reshape(c, w)


def _gla_direction(q, k, v, a, st, rev):
    c = q.shape[0]
    row = lax.broadcasted_iota(jnp.int32, (c, 1), 0)
    ri = lax.broadcasted_iota(jnp.int32, (c, c), 0)
    ci = lax.broadcasted_iota(jnp.int32, (c, c), 1)
    qk = jnp.sum(q * k, axis=-1, keepdims=True)
    amat = jnp.where(ri == ci, qk, 0.0)
    pq = a
    pk = jnp.ones_like(a)
    half = 1
    while half < c:
        blk = 2 * half
        in_second = (row & half) != 0
        q_side = in_second if not rev else jnp.logical_not(in_second)
        qs = jnp.where(q_side, q * pq, 0.0).astype(BF16)
        ks = jnp.where(q_side, 0.0, k * pk).astype(BF16)
        s = lax.dot_general(qs, ks, (((1,), (1,)), ((), ())), preferred_element_type=F32)
        amat = amat + (jnp.where((ri // blk) == (ci // blk), s, 0.0) if blk < c else s)
        if blk >= SUBLANES:
            if not rev:
                t_near = _block_row(pq, blk, half - 1)
                t_far = _block_row(pq, blk, blk - 1)
            else:
                t_near = _block_row(pq, blk, half)
                t_far = _block_row(pq, blk, 0)
        else:
            t_near = jnp.zeros_like(pq)
            t_far = jnp.zeros_like(pq)
            pos = row & (blk - 1)
            for u in range(blk):
                if (u >= half) != rev:
                    src = (half - 1) if not rev else half
                    t_near = jnp.where(pos == u, pltpu.roll(pq, (u - src) % c, 0), t_near)
                else:
                    src = (blk - 1) if not rev else 0
                    t_far = jnp.where(pos == u, pltpu.roll(pq, (u - src) % c, 0), t_far)
        pq_new = jnp.where(q_side, pq * t_near, pq)
        pk = jnp.where(q_side, pk, pk * t_far)
        pq = pq_new
        half = blk
    o = jnp.dot(amat.astype(BF16), v, preferred_element_type=F32)
    o = o + lax.dot_general((q * pq).astype(BF16), st.astype(BF16), (((1,), (1,)), ((), ())),
                            preferred_element_type=F32)
    total = pq[c - 1:c, :] if not rev else pq[0:1, :]
    kv = lax.dot_general(v, (k * pk).astype(BF16), (((0,), (0,)), ((), ())), preferred_element_type=F32)
    return o, st * total + kv


GLA_HEADS_PER_STEP = 4


def _gla_body(qf_ref, kf_ref, vf_ref, lf_ref, qb_ref, kb_ref, vb_ref, lb_ref, up_ref, bias_ref, s0_ref,
              of_ref, ob_ref, sout_ref, st_ref):
    j = pl.program_id(2)

    @pl.when(j == 0)
    def _():
        st_ref[...] = s0_ref[0]

    scale = GLA_DK ** -0.5
    for d, (q_ref, k_ref, v_ref, l_ref, o_ref) in enumerate(((qf_ref, kf_ref, vf_ref, lf_ref, of_ref),
                                                             (qb_ref, kb_ref, vb_ref, lb_ref, ob_ref))):
        x = jnp.dot(l_ref[...], up_ref[d], precision=lax.Precision.HIGHEST, preferred_element_type=F32) + bias_ref[d]
        g = (jnp.minimum(x, 0.0) - jnp.log(1.0 + jnp.exp(-jnp.abs(x)))) * (1.0 / GLA_GATE_NORM)
        a = jnp.exp(g)
        for hh in range(GLA_HEADS_PER_STEP):
            ks = slice(hh * GLA_DK, (hh + 1) * GLA_DK)
            vs = slice(hh * GLA_DV, (hh + 1) * GLA_DV)
            q = q_ref[:, ks].astype(F32) * scale
            k = k_ref[:, ks].astype(F32)
            o, st_new = _gla_direction(q, k, v_ref[:, vs], a[:, ks], st_ref[d, hh], rev=(d == 1))
            o_ref[:, vs] = o.astype(o_ref.dtype)
            st_ref[d, hh] = st_new

    @pl.when(j == pl.num_programs(2) - 1)
    def _():
        sout_ref[0] = st_ref[...]


def _gla(q, k, v, lr, up_pad, bias, s0, batch):
    t = q.shape[0]
    seq = t // batch
    c = min(GLA_CHUNK, seq)
    nc = seq // c
    hps = GLA_HEADS_PER_STEP
    kw, vw = hps * GLA_DK, hps * GLA_DV
    fwd = lambda b, h, j: (b * nc + j, h)
    bwd = lambda b, h, j: (b * nc + nc - 1 - j, h)
    fwd0 = lambda b, h, j: (b * nc + j, 0)
    bwd0 = lambda b, h, j: (b * nc + nc - 1 - j, 0)
    st_spec = pl.BlockSpec((1, 2, hps, GLA_DV, GLA_DK), lambda b, h, j: (b, 0, h, 0, 0))
    return pl.pallas_call(
        _gla_body,
        out_shape=[jax.ShapeDtypeStruct((t, GLA_VW), BF16), jax.ShapeDtypeStruct((t, GLA_VW), BF16),
                   jax.ShapeDtypeStruct(s0.shape, F32)],
        grid=(batch, GLA_HEADS // hps, nc),
        in_specs=[pl.BlockSpec((c, kw), fwd), pl.BlockSpec((c, kw), fwd), pl.BlockSpec((c, vw), fwd),
                  pl.BlockSpec((c, LANES), fwd0),
                  pl.BlockSpec((c, kw), bwd), pl.BlockSpec((c, kw), bwd), pl.BlockSpec((c, vw), bwd),
                  pl.BlockSpec((c, LANES), bwd0),
                  pl.BlockSpec((2, LANES, kw), lambda b, h, j: (0, 0, h)),
                  pl.BlockSpec((2, 1, kw), lambda b, h, j: (0, 0, h)),
                  st_spec],
        out_specs=[pl.BlockSpec((c, vw), fwd), pl.BlockSpec((c, vw), bwd), st_spec],
        scratch_shapes=[pltpu.VMEM((2, hps, GLA_DV, GLA_DK), F32)],
        compiler_params=_cparams(("parallel", "parallel", "arbitrary")),
        name="gla_scan",
    )(q, k, v, lr, q, k, v, lr, up_pad, bias, s0)


FFT_P = 128


def _fft_tables(seq):
    q = seq // FFT_P
    n_hi = np.arange(q)
    ang1 = 2.0 * np.pi * ((n_hi[:, None] * n_hi[None, :]) % q) / q
    w1 = np.concatenate([np.cos(ang1), -np.sin(ang1)], axis=0)
    n_lo = np.arange(FFT_P)
    k_a = np.arange(FFT_P)
    k_b = np.arange(q)
    k_full = k_b[:, None, None] + q * k_a[None, :, None]
    ang2 = 2.0 * np.pi * ((k_full * n_lo[None, None, :]) % seq) / seq
    c2, s2 = np.cos(ang2), np.sin(ang2)
    w2 = np.concatenate([np.concatenate([c2, s2], axis=2), np.concatenate([-s2, c2], axis=2)], axis=1)
    ch = np.arange(FNET_GROUP_W)
    angc = 2.0 * np.pi * ((ch[:, None] * ch[None, :]) % FNET_GROUP_W) / FNET_GROUP_W
    scale = 1.0 / np.sqrt(float(seq) * FNET_GROUP_W)
    eye = np.eye(FNET_GROUPS)
    wc = np.concatenate([np.kron(eye, np.cos(angc)), np.kron(eye, np.sin(angc))], axis=0) * scale
    return tuple(jnp.asarray(w, F32).astype(BF16) for w in (w1, w2, wc))


def _fft1_body(z_ref, w_ref, o_ref):
    o_ref[0] = jnp.dot(w_ref[...], z_ref[0], preferred_element_type=F32).astype(o_ref.dtype)


FFT_KB = 4


def _fft2_body(a_ref, w_ref, wc_ref, o_ref):
    for i in range(a_ref.shape[2]):
        a = a_ref[0, :, i].reshape(2 * FFT_P, FNET_W)
        z = jnp.dot(w_ref[i], a, preferred_element_type=F32).astype(BF16)
        zz = jnp.concatenate([z[:FFT_P], z[FFT_P:]], axis=-1)
        o_ref[0, :, i * FNET_W:(i + 1) * FNET_W] = jnp.dot(zz, wc_ref[...],
                                                           preferred_element_type=F32).astype(o_ref.dtype)


def _fourier_mix(f, batch):
    t = f.shape[0]
    seq = t // batch
    q = seq // FFT_P
    w1, w2, wc = _fft_tables(seq)
    cols = FFT_P * FNET_W
    tn = min(8192, cols)
    z = f.reshape(batch, q, cols)
    a = pl.pallas_call(
        _fft1_body,
        out_shape=jax.ShapeDtypeStruct((batch, 2 * q, cols), BF16),
        grid=(batch, cols // tn),
        in_specs=[pl.BlockSpec((1, q, tn), lambda b, j: (b, 0, j)),
                  pl.BlockSpec((2 * q, q), lambda b, j: (0, 0))],
        out_specs=pl.BlockSpec((1, 2 * q, tn), lambda b, j: (b, 0, j)),
        compiler_params=_cparams(("parallel", "parallel")),
        name="fft_stage1",
    )(z, w1)
    a = a.reshape(batch, 2, q, FFT_P, FNET_W)
    kbs = min(FFT_KB, q)
    out = pl.pallas_call(
        _fft2_body,
        out_shape=jax.ShapeDtypeStruct((batch, FFT_P, q * FNET_W), BF16),
        grid=(batch, q // kbs),
        in_specs=[pl.BlockSpec((1, 2, kbs, FFT_P, FNET_W), lambda b, kb: (b, 0, kb, 0, 0)),
                  pl.BlockSpec((kbs, 2 * FFT_P, 2 * FFT_P), lambda b, kb: (kb, 0, 0)),
                  pl.BlockSpec((2 * FNET_W, FNET_W), lambda b, kb: (0, 0))],
        out_specs=pl.BlockSpec((1, FFT_P, kbs * FNET_W), lambda b, kb: (b, 0, kb)),
        compiler_params=_cparams(("parallel", "parallel")),
        name="fft_stage2",
    )(a, w2, wc)
    return out.reshape(batch, FFT_P, q, FNET_W).reshape(t, FNET_W)


def _rms(y, g):
    return y * lax.rsqrt(jnp.mean(y * y, axis=-1, keepdims=True) + EPS) * g


def _even_out_body(of_ref, ob_ref, r_ref, fo_ref, x_ref, mod_ref, gn_ref, w_ref, gpost_ref, o_ref):
    parts = []
    for h in range(GLA_HEADS):
        sl = slice(h * GLA_DV, (h + 1) * GLA_DV)
        o = of_ref[:, sl].astype(F32) + ob_ref[:, sl].astype(F32)
        r = r_ref[:, sl].astype(F32)
        parts.append((_rms(o, gn_ref[...]) * (r * jax.nn.sigmoid(r))).astype(BF16))
    y = jnp.dot(jnp.concatenate(parts, axis=-1), w_ref[:GLA_VW, :], preferred_element_type=F32)
    y = y + jnp.dot(fo_ref[...], w_ref[GLA_VW:, :], preferred_element_type=F32)
    o_ref[...] = x_ref[...] + mod_ref[0, 2:3, :] * _rms(y, gpost_ref[...])


def _even_out(o_f, o_b, r, fo, x2, mod, gnorm, w_out, g_post, batch):
    t, d = x2.shape
    tm = min(ROW_TILE, t)
    tiles_per_mod = (t // batch) // tm
    row = lambda w: pl.BlockSpec((tm, w), lambda i: (i, 0))
    full = lambda a: pl.BlockSpec(a.shape, lambda i: (0,) * a.ndim)
    gn = gnorm.reshape(1, GLA_DV)
    gp = g_post.reshape(1, d)
    return pl.pallas_call(
        _even_out_body,
        out_shape=jax.ShapeDtypeStruct((t, d), F32),
        grid=(t // tm,),
        in_specs=[row(GLA_VW), row(GLA_VW), row(GLA_VW), row(FNET_W), row(d),
                  pl.BlockSpec((1, 8, d), lambda i: (i // tiles_per_mod, 0, 0)),
                  full(gn), full(w_out), full(gp)],
        out_specs=row(d),
        compiler_params=_cparams(("parallel",)),
        name="even_out",
    )(o_f, o_b, r, fo, x2, mod, gn, w_out, gp)


MOE_TILE = 512
MOE_BLOCK = 512
NEG_BIG = -1e30


def _route_body(x_ref, mod_ref, g_ref, rw_ref, rb_ref, h_ref, lpos_ref, gate_ref, cnt_ref, cbase_ref, base_ref):
    i = pl.program_id(0)

    @pl.when(i == 0)
    def _():
        base_ref[...] = jnp.zeros_like(base_ref)

    tm = x_ref.shape[0]
    h = _modnorm(x_ref[...], g_ref[...], mod_ref, 3, 4)
    h_hi = h.astype(BF16)
    h_ref[...] = h_hi
    h_lo = (h - h_hi.astype(F32)).astype(BF16)
    both = jnp.dot(h_hi, rw_ref[...], preferred_element_type=F32)
    logits = (both[:, :LANES] + both[:, LANES:]
              + jnp.dot(h_lo, rw_ref[:, :LANES], preferred_element_type=F32) + rb_ref[...])
    lane = lax.broadcasted_iota(jnp.int32, (tm, LANES), 1).astype(F32)
    work = logits
    vals, hots = [], []
    for _k in range(TOP_K):
        m = jnp.max(work, axis=-1, keepdims=True)
        idx = jnp.min(jnp.where(work == m, lane, float(LANES)), axis=-1, keepdims=True)
        hot = lane == idx
        vals.append(m)
        hots.append(hot)
        work = jnp.where(hot, -jnp.inf, work)
    es = [jnp.exp(v - vals[0]) for v in vals]
    inv = 1.0 / (es[0] + es[1] + es[2] + es[3])
    sel = sum(hh.astype(F32) for hh in hots)
    ri = lax.broadcasted_iota(jnp.int32, (tm, tm), 0)
    ci = lax.broadcasted_iota(jnp.int32, (tm, tm), 1)
    earlier = jnp.where(ci < ri, 1.0, 0.0).astype(BF16)
    cum = jnp.dot(earlier, sel.astype(BF16), preferred_element_type=F32)
    cnt = jnp.sum(sel, axis=0, keepdims=True)
    ei = lax.broadcasted_iota(jnp.int32, (LANES, LANES), 0)
    ej = lax.broadcasted_iota(jnp.int32, (LANES, LANES), 1)
    before = jnp.where(ei < ej, 1.0, 0.0)
    loff = jnp.dot(jnp.broadcast_to(cnt, (SUBLANES, LANES)), before, precision=lax.Precision.HIGHEST,
                   preferred_element_type=F32)[0:1]
    where_to = cum + loff
    for k in range(TOP_K):
        p = jnp.sum(jnp.where(hots[k], where_to, 0.0), axis=-1, keepdims=True)
        lpos_ref[:, k:k + 1] = p.astype(jnp.int32)
        gate_ref[:, k:k + 1] = es[k] * inv
    cnt_ref[0] = cnt.astype(jnp.int32)
    cbase_ref[0] = base_ref[...].astype(jnp.int32)
    base_ref[...] = base_ref[...] + cnt


def _route(x2, mod, g_pre, rw, rb, batch):
    t, d = x2.shape
    tm = min(MOE_TILE, t)
    nt = t // tm
    tiles_per_mod = (t // batch) // tm
    rw_pad = jnp.zeros((d, LANES), F32).at[:, :N_EXPERTS].set(rw)
    rw_hi = rw_pad.astype(BF16)
    rw_pad = jnp.concatenate([rw_hi, (rw_pad - rw_hi.astype(F32)).astype(BF16)], axis=1)
    rb_pad = jnp.full((1, LANES), NEG_BIG, F32).at[0, :N_EXPERTS].set(rb)
    return pl.pallas_call(
        _route_body,
        out_shape=[jax.ShapeDtypeStruct((t, d), BF16), jax.ShapeDtypeStruct((t, TOP_K), jnp.int32),
                   jax.ShapeDtypeStruct((t, TOP_K), F32), jax.ShapeDtypeStruct((nt, 1, LANES), jnp.int32),
                   jax.ShapeDtypeStruct((nt, 1, LANES), jnp.int32)],
        grid=(nt,),
        in_specs=[pl.BlockSpec((tm, d), lambda i: (i, 0)),
                  pl.BlockSpec((1, 8, d), lambda i: (i // tiles_per_mod, 0, 0)),
                  pl.BlockSpec((1, d), lambda i: (0, 0)),
                  pl.BlockSpec((d, 2 * LANES), lambda i: (0, 0)),
                  pl.BlockSpec((1, LANES), lambda i: (0, 0))],
        out_specs=[pl.BlockSpec((tm, d), lambda i: (i, 0)), pl.BlockSpec((tm, TOP_K), lambda i: (i, 0)),
                   pl.BlockSpec((tm, TOP_K), lambda i: (i, 0)), pl.BlockSpec((1, 1, LANES), lambda i: (i, 0, 0)),
                   pl.BlockSpec((1, 1, LANES), lambda i: (i, 0, 0))],
        scratch_shapes=[pltpu.VMEM((1, LANES), F32)],
        compiler_params=_cparams(("arbitrary",)),
        name="moe_route",
    )(x2, mod, g_pre.reshape(1, d), rw_pad, rb_pad)


def _moe_plan(cnt, cbase, n_slots):
    cnt = cnt[:, 0, :N_EXPERTS]
    cbase = cbase[:, 0, :N_EXPERTS]
    total = cbase[-1] + cnt[-1]
    padded = (total + MOE_BLOCK - 1) // MOE_BLOCK * MOE_BLOCK
    pad_end = jnp.cumsum(padded)
    start = pad_end - padded
    n_blocks = (n_slots + N_EXPERTS * (MOE_BLOCK - 1) + MOE_BLOCK - 1) // MOE_BLOCK
    n_used = pad_end[-1] // MOE_BLOCK
    blk = jnp.arange(n_blocks, dtype=jnp.int32)
    first_row = jnp.minimum(blk, n_used - 1) * MOE_BLOCK
    block_e = jnp.minimum(jnp.sum((pad_end[None, :] <= first_row[:, None]).astype(jnp.int32), axis=1), N_EXPERTS - 1)
    loff = jnp.cumsum(cnt, axis=1) - cnt
    gdst = start[None, :] + cbase
    flat = lambda a: a.reshape(-1).astype(jnp.int32)
    fill = (flat(padded - total), jnp.zeros((N_EXPERTS,), jnp.int32), flat(start + total))
    return (n_blocks, block_e, n_used.reshape(1).astype(jnp.int32), flat(cnt), flat(loff), flat(gdst)) + fill


ROW_SUB = D_MODEL // LANES


def _rows_to_tiles(ref, value):
    n = value.shape[0]
    for j in range(ROW_SUB):
        ref[pl.ds(j, n, stride=ROW_SUB), :] = value[:, j * LANES:(j + 1) * LANES]


def _tiles_to_rows(ref):
    n = ref.shape[0] // ROW_SUB
    return jnp.concatenate([ref[pl.ds(j, n, stride=ROW_SUB), :] for j in range(ROW_SUB)], axis=-1)


def _strip_copies(cnt_s, loff_s, gdst_s, tile, local_ref, global_ref, sem, to_global, max_rows):
    span = lambda start, size: pl.ds(pl.multiple_of(start * ROW_SUB, ROW_SUB), size * ROW_SUB)

    def per_expert(e, carry):
        n = cnt_s[tile * N_EXPERTS + e]
        src0 = loff_s[tile * N_EXPERTS + e]
        dst0 = gdst_s[tile * N_EXPERTS + e]
        off = jnp.int32(0)
        size = max_rows
        while size >= 1:
            take = n & size

            @pl.when(take != 0)
            def _(off=off, size=size):
                loc = local_ref.at[span(src0 + off, size)]
                glo = global_ref.at[span(dst0 + off, size)]
                if to_global:
                    pltpu.make_async_copy(loc, glo, sem).start()
                else:
                    pltpu.make_async_copy(glo, loc, sem).start()

            off = off + take
            size //= 2
        return carry

    lax.fori_loop(0, N_EXPERTS, per_expert, 0)


def _dispatch_body(cnt_s, loff_s, gdst_s, zcnt_s, zoff_s, zdst_s, nu_s, h_ref, lpos_ref, xb_ref, xs0_ref, xs1_ref, zero_ref,
                   sem0, sem1, zsem, *, n_blocks, n_fill, n_tiles):
    i = pl.program_id(0)
    tm = h_ref.shape[0]
    rows = TOP_K * tm
    lane = lax.broadcasted_iota(jnp.int32, (tm, rows), 1)
    onehot = jnp.zeros((tm, rows), F32)
    for k in range(TOP_K):
        onehot = jnp.where(lane == lpos_ref[:, k:k + 1], 1.0, onehot)
    xs = lax.dot_general(onehot.astype(BF16), h_ref[...], (((0,), (0,)), ((), ())), preferred_element_type=F32)
    slots = ((xs0_ref, sem0), (xs1_ref, sem1))

    def drain(slot):
        xs_ref, sem = slots[slot]
        pltpu.make_async_copy(xs_ref, xb_ref.at[pl.ds(0, rows * ROW_SUB)], sem).wait()

    for slot in range(2):
        @pl.when(i % 2 == slot)
        def _(slot=slot):
            @pl.when(i >= 2)
            def _():
                drain(slot)

            xs_ref, sem = slots[slot]
            _rows_to_tiles(xs_ref, xs)
            _strip_copies(cnt_s, loff_s, gdst_s, i, xs_ref, xb_ref, sem, True, tm)

    @pl.when(i == n_tiles - 1)
    def _():
        drain((n_tiles - 1) % 2)
        if n_tiles >= 2:
            drain(n_tiles % 2)
        zero_ref[...] = jnp.zeros_like(zero_ref)
        _strip_copies(zcnt_s, zoff_s, zdst_s, 0, zero_ref, xb_ref, zsem, True, MOE_BLOCK // 2)

        def per_block(b, carry):
            @pl.when(b >= nu_s[0])
            def _():
                pltpu.make_async_copy(zero_ref, xb_ref.at[pl.ds(b * (MOE_BLOCK * ROW_SUB), MOE_BLOCK * ROW_SUB)],
                                      zsem).start()
            return carry

        lax.fori_loop(0, n_blocks, per_block, 0)
        pltpu.make_async_copy(xb_ref.at[pl.ds(0, n_fill * ROW_SUB)], xb_ref.at[pl.ds(0, n_fill * ROW_SUB)], zsem).wait()


def _dispatch(h, lpos, plan):
    n_blocks, _, n_used, cnt_s, loff_s, gdst_s, zcnt_s, zoff_s, zdst_s = plan
    t, d = h.shape
    tm = min(MOE_TILE, t)
    rows = TOP_K * tm
    n_fill = n_blocks * MOE_BLOCK - TOP_K * t
    return pl.pallas_call(
        functools.partial(_dispatch_body, n_blocks=n_blocks, n_fill=n_fill, n_tiles=t // tm),
        out_shape=jax.ShapeDtypeStruct((n_blocks * MOE_BLOCK * ROW_SUB, LANES), F32),
        grid_spec=pltpu.PrefetchScalarGridSpec(
            num_scalar_prefetch=7,
            grid=(t // tm,),
            in_specs=[pl.BlockSpec((tm, d), lambda i, *_: (i, 0)),
                      pl.BlockSpec((tm, TOP_K), lambda i, *_: (i, 0))],
            out_specs=pl.BlockSpec(memory_space=pl.ANY),
            scratch_shapes=[pltpu.VMEM((rows * ROW_SUB, LANES), F32), pltpu.VMEM((rows * ROW_SUB, LANES), F32),
                            pltpu.VMEM((MOE_BLOCK * ROW_SUB, LANES), F32),
                            pltpu.SemaphoreType.DMA, pltpu.SemaphoreType.DMA, pltpu.SemaphoreType.DMA],
        ),
        compiler_params=_cparams(("arbitrary",)),
        name="moe_dispatch",
    )(cnt_s, loff_s, gdst_s, zcnt_s, zoff_s, zdst_s, n_used, h, lpos)


def _experts_body(be_s, nu_s, x_ref, wgu_ref, bgu_ref, wd_ref, bd_ref, o_ref, wgu_bf, wd_bf):
    i = pl.program_id(0)

    @pl.when(i < nu_s[0])
    def _():
        @pl.when((i == 0) | (be_s[i] != be_s[jnp.maximum(i - 1, 0)]))
        def _():
            wgu_bf[...] = wgu_ref[0].astype(BF16)
            wd_bf[...] = wd_ref[0].astype(BF16)

        x = _tiles_to_rows(x_ref).astype(BF16)
        gu = jnp.dot(x, wgu_bf[...], preferred_element_type=F32) + bgu_ref[0]
        g = jnp.minimum(gu[:, :D_FF], SWIGLU_LIMIT)
        u = jnp.clip(gu[:, D_FF:], -SWIGLU_LIMIT, SWIGLU_LIMIT)
        act = (u + 1.0) * (g * jax.nn.sigmoid(SWIGLU_ALPHA * g))
        y = jnp.dot(act.astype(BF16), wd_bf[...], preferred_element_type=F32) + bd_ref[0]
        _rows_to_tiles(o_ref, y)

    @pl.when(i >= nu_s[0])
    def _():
        o_ref[...] = jnp.zeros_like(o_ref)


def _experts(xb, plan, layer, w_gu, b_gu, w_down, b_down):
    n_blocks, block_e, n_used = plan[:3]
    d = D_MODEL
    base = layer * N_EXPERTS
    flat = lambda a: a.reshape((-1,) + a.shape[2:])
    slot_block = pl.BlockSpec((MOE_BLOCK * ROW_SUB, LANES), lambda i, be, nu: (i, 0))
    return pl.pallas_call(
        _experts_body,
        out_shape=jax.ShapeDtypeStruct(xb.shape, F32),
        grid_spec=pltpu.PrefetchScalarGridSpec(
            num_scalar_prefetch=2,
            grid=(n_blocks,),
            in_specs=[slot_block,
                      pl.BlockSpec((1, d, 2 * D_FF), lambda i, be, nu: (base + be[i], 0, 0)),
                      pl.BlockSpec((1, 1, 2 * D_FF), lambda i, be, nu: (base + be[i], 0, 0)),
                      pl.BlockSpec((1, D_FF, d), lambda i, be, nu: (base + be[i], 0, 0)),
                      pl.BlockSpec((1, 1, d), lambda i, be, nu: (base + be[i], 0, 0))],
            out_specs=slot_block,
            scratch_shapes=[pltpu.VMEM((d, 2 * D_FF), BF16), pltpu.VMEM((D_FF, d), BF16)],
        ),
        compiler_params=_cparams(("arbitrary",)),
        name="moe_experts",
    )(block_e, n_used, xb, flat(w_gu), flat(b_gu)[:, None, :], flat(w_down), flat(b_down)[:, None, :])


def _combine_body(cnt_s, loff_s, gdst_s, yb_ref, lpos_ref, gate_ref, x_ref, mod_ref, g_ref, o_ref, ys0_ref, ys1_ref,
                  ys2_ref, sem0, sem1, *, n_tiles):
    i = pl.program_id(0)
    tm = x_ref.shape[0]
    rows = TOP_K * tm
    slots = ((ys0_ref, sem0), (ys1_ref, sem1))

    def fetch(tile, slot):
        ys_ref, sem = slots[slot]
        _strip_copies(cnt_s, loff_s, gdst_s, tile, ys_ref, yb_ref, sem, False, tm)

    @pl.when(i == 0)
    def _():
        fetch(0, 0)

    for slot in range(2):
        @pl.when(i % 2 == slot)
        def _(slot=slot):
            @pl.when(i + 1 < n_tiles)
            def _():
                fetch(i + 1, 1 - slot)

            ys_ref, sem = slots[slot]
            pltpu.make_async_copy(yb_ref.at[pl.ds(0, rows * ROW_SUB)], ys_ref, sem).wait()
            ys2_ref[...] = _tiles_to_rows(ys_ref).astype(BF16)

    lane = lax.broadcasted_iota(jnp.int32, (tm, rows), 1)
    weights = jnp.zeros((tm, rows), F32)
    for k in range(TOP_K):
        weights = jnp.where(lane == lpos_ref[:, k:k + 1], gate_ref[:, k:k + 1], weights)
    f = jnp.dot(weights.astype(BF16), ys2_ref[...], preferred_element_type=F32)
    o_ref[...] = x_ref[...] + mod_ref[0, 5:6, :] * _rms(f, g_ref[...])


def _combine(yb, lpos, gate, x2, mod, g_post, plan, batch):
    cnt_s, loff_s, gdst_s = plan[3:6]
    t, d = x2.shape
    tm = min(MOE_TILE, t)
    rows = TOP_K * tm
    tiles_per_mod = (t // batch) // tm
    return pl.pallas_call(
        functools.partial(_combine_body, n_tiles=t // tm),
        out_shape=jax.ShapeDtypeStruct((t, d), F32),
        grid_spec=pltpu.PrefetchScalarGridSpec(
            num_scalar_prefetch=3,
            grid=(t // tm,),
            in_specs=[pl.BlockSpec(memory_space=pl.ANY),
                      pl.BlockSpec((tm, TOP_K), lambda i, *_: (i, 0)),
                      pl.BlockSpec((tm, TOP_K), lambda i, *_: (i, 0)),
                      pl.BlockSpec((tm, d), lambda i, *_: (i, 0)),
                      pl.BlockSpec((1, 8, d), lambda i, *_: (i // tiles_per_mod, 0, 0)),
                      pl.BlockSpec((1, d), lambda i, *_: (0, 0))],
            out_specs=pl.BlockSpec((tm, d), lambda i, *_: (i, 0)),
            scratch_shapes=[pltpu.VMEM((rows * ROW_SUB, LANES), F32), pltpu.VMEM((rows * ROW_SUB, LANES), F32),
                            pltpu.VMEM((rows, d), BF16),
                            pltpu.SemaphoreType.DMA, pltpu.SemaphoreType.DMA],
        ),
        compiler_params=_cparams(("arbitrary",)),
        name="moe_combine",
    )(cnt_s, loff_s, gdst_s, yb, lpos, gate, x2, mod, g_post.reshape(1, d))


def _moe_ffn(x2, mod, g_pre, g_post, rw, rb, layer, w_gu, b_gu, w_down, b_down, batch):
    h, lpos, gate, cnt, cbase = _route(x2, mod, g_pre, rw, rb, batch)
    plan = _moe_plan(cnt, cbase, x2.shape[0] * TOP_K)
    xb = _dispatch(h, lpos, plan)
    yb = _experts(xb, plan, layer, w_gu, b_gu, w_down, b_down)
    return _combine(yb, lpos, gate, x2, mod, g_post, plan, batch)


CONF_HALO = 16


def _odd_body(a_ref, ap_ref, an_ref, gb_ref, cu_ref, cup_ref, cun_ref, x_ref, mod_ref, dw_ref, dwb_ref, lng_ref,
              lnb_ref, sc_ref, w_ref, gpost_ref, o_ref, shift_ref, *, tiles_per_seq):
    i = pl.program_id(0)
    tm = x_ref.shape[0]
    first = (i % tiles_per_seq) == 0
    last = (i % tiles_per_seq) == tiles_per_seq - 1

    def glu(ref):
        v = ref[...].astype(F32)
        return v[:, :CONF_W] * jax.nn.sigmoid(v[:, CONF_W:])

    pad = CONF_KERNEL // 2
    ext = jnp.concatenate([jnp.where(first, 0.0, glu(ap_ref)), glu(a_ref), jnp.where(last, 0.0, glu(an_ref))], axis=0)
    acc = jnp.zeros((tm, CONF_W), F32) + dwb_ref[...]
    span = shift_ref.shape[1]
    for r in range(SUBLANES):
        shift_ref[r] = ext[r:r + span, :]
    for k in range(CONF_KERNEL):
        lo = CONF_HALO - pad + k
        base = (lo // SUBLANES) * SUBLANES
        acc = acc + dw_ref[k:k + 1, :] * shift_ref[lo % SUBLANES, base:base + tm, :]
    mu = jnp.mean(acc, axis=-1, keepdims=True)
    xc = acc - mu
    hn = xc * lax.rsqrt(jnp.mean(xc * xc, axis=-1, keepdims=True) + EPS) * lng_ref[...] + lnb_ref[...]
    hc = (hn * jax.nn.sigmoid(hn)).astype(BF16)

    def gated(ref):
        v = ref[...].astype(F32)
        return v[:, :SCONV_W] * v[:, SCONV_W:]

    zc = gated(cu_ref)
    zext = jnp.concatenate([jnp.where(first, 0.0, gated(cup_ref)), zc, jnp.where(last, 0.0, gated(cun_ref))], axis=0)
    z = sc_ref[0:1, :] * zext[0:tm] + sc_ref[1:2, :] * zc + sc_ref[2:3, :] * zext[2 * GRID_W:2 * GRID_W + tm]
    z = (gb_ref[...].astype(F32) * z).astype(BF16)
    y = jnp.dot(hc, w_ref[:CONF_W, :], preferred_element_type=F32) + jnp.dot(z, w_ref[CONF_W:, :],
                                                                            preferred_element_type=F32)
    o_ref[...] = x_ref[...] + mod_ref[0, 2:3, :] * _rms(y, gpost_ref[...])


def _odd_mix(a, gb, cu, x2, mod, conf_dw, conf_dw_b, ln_g, ln_b, sconv, w_out, g_post, batch):
    t, d = x2.shape
    seq = t // batch
    tm = min(ROW_TILE, seq)
    tiles_per_seq = seq // tm
    nt = t // tm
    hb_c, hb_s = tm // CONF_HALO, tm // GRID_W
    row = lambda w: pl.BlockSpec((tm, w), lambda i: (i, 0))
    prev = lambda rows, per, w: pl.BlockSpec((rows, w), lambda i: (jnp.maximum(i * per - 1, 0), 0))
    nxt = lambda rows, per, w: pl.BlockSpec((rows, w), lambda i: (jnp.minimum((i + 1) * per, nt * per - 1), 0))
    full = lambda arr: pl.BlockSpec(arr.shape, lambda i: (0,) * arr.ndim)
    smalls = [conf_dw, conf_dw_b.reshape(1, -1), ln_g.reshape(1, -1), ln_b.reshape(1, -1), sconv, w_out,
              g_post.reshape(1, d)]
    return pl.pallas_call(
        functools.partial(_odd_body, tiles_per_seq=tiles_per_seq),
        out_shape=jax.ShapeDtypeStruct((t, d), F32),
        grid=(nt,),
        in_specs=[row(2 * CONF_W), prev(CONF_HALO, hb_c, 2 * CONF_W), nxt(CONF_HALO, hb_c, 2 * CONF_W),
                  row(SCONV_W),
                  row(2 * SCONV_W), prev(GRID_W, hb_s, 2 * SCONV_W), nxt(GRID_W, hb_s, 2 * SCONV_W),
                  row(d), pl.BlockSpec((1, 8, d), lambda i: (i // tiles_per_seq, 0, 0))] + [full(s) for s in smalls],
        out_specs=row(d),
        scratch_shapes=[pltpu.VMEM((SUBLANES, tm + CONF_HALO + SUBLANES, CONF_W), F32)],
        compiler_params=_cparams(("parallel",)),
        name="odd_mix",
    )(a, a, a, gb, cu, cu, cu, x2, mod, *smalls)


COND_ROWS = 16


def _mod_rows(mods_layer):
    m = mods_layer.reshape(COND_ROWS, 6, D_MODEL)
    return jnp.concatenate([m, jnp.zeros((COND_ROWS, 2, D_MODEL), F32)], axis=1)


def _even_weights(w_in, gk_up, gk_b):
    cuts = np.cumsum((GLA_KW, GLA_KW, GLA_VW, GLA_VW, GLA_LOWRANK, GLA_LOWRANK, FNET_W))[:-1]
    wq, wk, wv, wr, wlf, wlb, wf = jnp.split(w_in, [int(v) for v in cuts], axis=-1)
    w_main = jnp.concatenate([wq, wk, wv, wr, wf], axis=-1).astype(BF16)
    pad = jnp.zeros((w_in.shape[0], LANES - 2 * GLA_LOWRANK), w_in.dtype)
    w_tail = jnp.concatenate([wlf, wlb, pad], axis=-1).astype(BF16)
    up_pad = jnp.zeros((2, LANES, GLA_KW), F32)
    up_pad = up_pad.at[0, :GLA_LOWRANK].set(gk_up[0]).at[1, GLA_LOWRANK:2 * GLA_LOWRANK].set(gk_up[1])
    return w_main, w_tail, up_pad, gk_b.reshape(2, 1, GLA_KW)


EVEN_WIDTHS = (GLA_KW, GLA_KW, GLA_VW, GLA_VW, FNET_W)


def _even_gla(x2, ctx2, mod, g_pre, w_main, w_tail, up_pad, bias, batch):
    nb = batch
    cq, ck, cv, _, _, clr = _norm_proj(ctx2, mod[nb:nb + 1], ctx2.shape[0], g_pre, w_main, EVEN_WIDTHS, w_tail)
    zero = jnp.zeros((nb, 2, GLA_HEADS, GLA_DV, GLA_DK), F32)
    _, _, s_ctx = _gla(cq, ck, cv, clr, up_pad, bias, zero, nb)
    q, k, v, r, f, lr = _norm_proj(x2, mod[:nb], x2.shape[0] // nb, g_pre, w_main, EVEN_WIDTHS, w_tail)
    o_f, o_b, _ = _gla(q, k, v, lr, up_pad, bias, s_ctx, nb)
    return r, f, o_f, o_b


def kernel(x, c, ctx, c_ctx, mod_w, mod_b, norm_mix_pre, norm_mix_post, norm_ffn_pre, norm_ffn_post, ev_w_in, ev_gk_up, ev_gk_b, ev_gnorm, ev_w_out, od_w_in, od_conf_dw, od_conf_dw_b, od_conf_ln_g, od_conf_ln_b, od_sconv, od_w_out, router_w, router_b, exp_w_gu, exp_b_gu, exp_w_down, exp_b_down):
    nb, seq, d = x.shape
    cond = jnp.concatenate([c, c_ctx[None], jnp.zeros((COND_ROWS - nb - 1, d), F32)], axis=0)
    mods = _adaln(cond, mod_w, mod_b)
    x2 = x.reshape(nb * seq, d)
    ctx2 = ctx.reshape(nb * ctx.shape[1], d)
    depth = mod_w.shape[0]
    assert depth == 2, "layer pattern implemented for one even layer followed by one odd layer"
    mod = _mod_rows(mods[0])
    wm, wt, up, bias = _even_weights(ev_w_in[0], ev_gk_up[0], ev_gk_b[0])
    r, f, o_f, o_b = _even_gla(x2, ctx2, mod, norm_mix_pre[0], wm, wt, up, bias, nb)
    fo = _fourier_mix(f, nb)
    x2 = _even_out(o_f, o_b, r, fo, x2, mod, ev_gnorm[0], ev_w_out[0].astype(BF16), norm_mix_post[0], nb)
    x2 = _moe_ffn(x2, mod, norm_ffn_pre[0], norm_ffn_post[0], router_w[0], router_b[0],
                  0, exp_w_gu, exp_b_gu, exp_w_down, exp_b_down, nb)
    mod = _mod_rows(mods[1])
    a, gb, cu = _norm_proj(x2, mod[:nb], seq, norm_mix_pre[1], od_w_in[0].astype(BF16),
                           (2 * CONF_W, SCONV_W, 2 * SCONV_W))
    x2 = _odd_mix(a, gb, cu, x2, mod, od_conf_dw[0], od_conf_dw_b[0], od_conf_ln_g[0], od_conf_ln_b[0], od_sconv[0],
                  od_w_out[0].astype(BF16), norm_mix_post[1], nb)
    x2 = _moe_ffn(x2, mod, norm_ffn_pre[1], norm_ffn_post[1], router_w[1], router_b[1],
                  1, exp_w_gu, exp_b_gu, exp_w_down, exp_b_down, nb)
    return x2.reshape(nb, seq, d)
```

```python
import functools

import numpy as np
import jax
import jax.numpy as jnp
from jax import lax
from jax.experimental import pallas as pl
from jax.experimental.pallas import tpu as pltpu

F32 = jnp.float32
BF16 = jnp.bfloat16

D_MODEL = 1024
EPS = 1e-6
GRID_W = 64
GLA_HEADS = 4
GLA_DK = 128
GLA_DV = 256
GLA_LOWRANK = 16
GLA_GATE_NORM = 16.0
GLA_KW = GLA_HEADS * GLA_DK
GLA_VW = GLA_HEADS * GLA_DV
FNET_GROUPS = 4
FNET_GROUP_W = 128
FNET_W = FNET_GROUPS * FNET_GROUP_W
CONF_W = 512
CONF_KERNEL = 31
SCONV_W = 512
N_EXPERTS = 32
TOP_K = 4
D_FF = D_MODEL
SWIGLU_LIMIT = 7.0
SWIGLU_ALPHA = 1.702

LANES = 128
SUBLANES = 8
VMEM_LIMIT = 56 * 1024 * 1024

GLA_CHUNK = 128
ROW_TILE = 512


def _cparams(sem):
    return pltpu.CompilerParams(dimension_semantics=sem, vmem_limit_bytes=VMEM_LIMIT)


def _adaln_body(c_ref, w_ref, b_ref, o_ref):
    c = c_ref[...]
    s = c * jax.nn.sigmoid(c)
    o_ref[0] = jnp.dot(s, w_ref[0], precision=lax.Precision.HIGHEST, preferred_element_type=F32) + b_ref[0]


def _adaln(cond, mod_w, mod_b):
    depth, d, n = mod_w.shape
    rows = cond.shape[0]
    tn = 1536
    return pl.pallas_call(
        _adaln_body,
        out_shape=jax.ShapeDtypeStruct((depth, rows, n), F32),
        grid=(depth, n // tn),
        in_specs=[pl.BlockSpec((rows, d), lambda l, j: (0, 0)),
                  pl.BlockSpec((1, d, tn), lambda l, j: (l, 0, j)),
                  pl.BlockSpec((1, 1, tn), lambda l, j: (l, 0, j))],
        out_specs=pl.BlockSpec((1, rows, tn), lambda l, j: (l, 0, j)),
        compiler_params=_cparams(("arbitrary", "arbitrary")),
        name="adaln",
    )(cond, mod_w, mod_b.reshape(depth, 1, n))


def _modnorm(x, g, mod_ref, shift_row, scale_row):
    ms = jnp.mean(x * x, axis=-1, keepdims=True)
    y = x * lax.rsqrt(ms + EPS) * g
    return y * (1.0 + mod_ref[0, scale_row:scale_row + 1, :]) + mod_ref[0, shift_row:shift_row + 1, :]


def _normproj_body(x_ref, mod_ref, g_ref, w_ref, *rest, widths, has_f32_tail):
    if has_f32_tail:
        wt_ref, outs = rest[0], rest[1:]
    else:
        wt_ref, outs = None, rest
    h = _modnorm(x_ref[...], g_ref[...], mod_ref, 0, 1).astype(BF16)
    col = 0
    for o_ref, wd in zip(outs, widths):
        step = 512
        for j in range(0, wd, step):
            o_ref[:, j:j + step] = jnp.dot(h, w_ref[:, col + j:col + j + step],
                                           preferred_element_type=F32).astype(o_ref.dtype)
        col += wd
    if has_f32_tail:
        outs[-1][...] = jnp.dot(h, wt_ref[...], preferred_element_type=F32)


def _norm_proj(x2, mod, rows_per_mod, g, w, widths, w_tail=None):
    t, d = x2.shape
    tm = min(ROW_TILE, t)
    n = w.shape[1]
    assert sum(widths) == n and all(wd % 512 == 0 for wd in widths) and rows_per_mod % tm == 0
    tiles_per_mod = rows_per_mod // tm
    in_specs = [pl.BlockSpec((tm, d), lambda i: (i, 0)),
                pl.BlockSpec((1, 8, d), lambda i: (i // tiles_per_mod, 0, 0)),
                pl.BlockSpec((1, d), lambda i: (0, 0)),
                pl.BlockSpec((d, n), lambda i: (0, 0))]
    args = [x2, mod, g.reshape(1, d), w]
    out_shape = [jax.ShapeDtypeStruct((t, wd), BF16) for wd in widths]
    out_specs = [pl.BlockSpec((tm, wd), lambda i: (i, 0)) for wd in widths]
    if w_tail is not None:
        in_specs.append(pl.BlockSpec((d, LANES), lambda i: (0, 0)))
        args.append(w_tail)
        out_shape.append(jax.ShapeDtypeStruct((t, LANES), F32))
        out_specs.append(pl.BlockSpec((tm, LANES), lambda i: (i, 0)))
    return pl.pallas_call(
        functools.partial(_normproj_body, widths=tuple(widths), has_f32_tail=w_tail is not None),
        out_shape=out_shape,
        grid=(t // tm,),
        in_specs=in_specs,
        out_specs=out_specs,
        compiler_params=_cparams(("parallel",)),
        name="norm_proj",
    )(*args)


def _block_row(x, blk, r):
    c, w = x.shape
    x3 = x.reshape(c // blk, blk, w)
    return jnp.broadcast_to(x3[:, r:r + 1, :], x3.shape).reshape(c, w)


def _gla_direction(q, k, v, a, st, rev):
    c = q.shape[0]
    row = lax.broadcasted_iota(jnp.int32, (c, 1), 0)
    ri = lax.broadcasted_iota(jnp.int32, (c, c), 0)
    ci = lax.broadcasted_iota(jnp.int32, (c, c), 1)
    qk = jnp.sum(q * k, axis=-1, keepdims=True)
    amat = jnp.where(ri == ci, qk, 0.0)
    pq = a
    pk = jnp.ones_like(a)
    half = 1
    while half < c:
        blk = 2 * half
        in_second = (row & half) != 0
        q_side = in_second if not rev else jnp.logical_not(in_second)
        qs = jnp.where(q_side, q * pq, 0.0).astype(BF16)
        ks = jnp.where(q_side, 0.0, k * pk).astype(BF16)
        s = lax.dot_general(qs, ks, (((1,), (1,)), ((), ())), preferred_element_type=F32)
        amat = amat + (jnp.where((ri // blk) == (ci // blk), s, 0.0) if blk < c else s)
        if blk >= SUBLANES:
            if not rev:
                t_near = _block_row(pq, blk, half - 1)
                t_far = _block_row(pq, blk, blk - 1)
            else:
                t_near = _block_row(pq, blk, half)
                t_far = _block_row(pq, blk, 0)
        else:
            t_near = jnp.zeros_like(pq)
            t_far = jnp.zeros_like(pq)
            pos = row & (blk - 1)
            for u in range(blk):
                if (u >= half) != rev:
                    src = (half - 1) if not rev else half
                    t_near = jnp.where(pos == u, pltpu.roll(pq, (u - src) % c, 0), t_near)
                else:
                    src = (blk - 1) if not rev else 0
                    t_far = jnp.where(pos == u, pltpu.roll(pq, (u - src) % c, 0), t_far)
        pq_new = jnp.where(q_side, pq * t_near, pq)
        pk = jnp.where(q_side, pk, pk * t_far)
        pq = pq_new
        half = blk
    o = jnp.dot(amat.astype(BF16), v, preferred_element_type=F32)
    o = o + lax.dot_general((q * pq).astype(BF16), st.astype(BF16), (((1,), (1,)), ((), ())),
                            preferred_element_type=F32)
    total = pq[c - 1:c, :] if not rev else pq[0:1, :]
    kv = lax.dot_general(v, (k * pk).astype(BF16), (((0,), (0,)), ((), ())), preferred_element_type=F32)
    return o, st * total + kv


GLA_HEADS_PER_STEP = 4


def _gla_body(qf_ref, kf_ref, vf_ref, lf_ref, qb_ref, kb_ref, vb_ref, lb_ref, up_ref, bias_ref, s0_ref,
              of_ref, ob_ref, sout_ref, st_ref):
    j = pl.program_id(2)

    @pl.when(j == 0)
    def _():
        st_ref[...] = s0_ref[0]

    scale = GLA_DK ** -0.5
    for d, (q_ref, k_ref, v_ref, l_ref, o_ref) in enumerate(((qf_ref, kf_ref, vf_ref, lf_ref, of_ref),
                                                             (qb_ref, kb_ref, vb_ref, lb_ref, ob_ref))):
        x = jnp.dot(l_ref[...], up_ref[d], precision=lax.Precision.HIGHEST, preferred_element_type=F32) + bias_ref[d]
        g = (jnp.minimum(x, 0.0) - jnp.log(1.0 + jnp.exp(-jnp.abs(x)))) * (1.0 / GLA_GATE_NORM)
        a = jnp.exp(g)
        for hh in range(GLA_HEADS_PER_STEP):
            ks = slice(hh * GLA_DK, (hh + 1) * GLA_DK)
            vs = slice(hh * GLA_DV, (hh + 1) * GLA_DV)
            q = q_ref[:, ks].astype(F32) * scale
            k = k_ref[:, ks].astype(F32)
            o, st_new = _gla_direction(q, k, v_ref[:, vs], a[:, ks], st_ref[d, hh], rev=(d == 1))
            o_ref[:, vs] = o.astype(o_ref.dtype)
            st_ref[d, hh] = st_new

    @pl.when(j == pl.num_programs(2) - 1)
    def _():
        sout_ref[0] = st_ref[...]


def _gla(q, k, v, lr, up_pad, bias, s0, batch):
    t = q.shape[0]
    seq = t // batch
    c = min(GLA_CHUNK, seq)
    nc = seq // c
    hps = GLA_HEADS_PER_STEP
    kw, vw = hps * GLA_DK, hps * GLA_DV
    fwd = lambda b, h, j: (b * nc + j, h)
    bwd = lambda b, h, j: (b * nc + nc - 1 - j, h)
    fwd0 = lambda b, h, j: (b * nc + j, 0)
    bwd0 = lambda b, h, j: (b * nc + nc - 1 - j, 0)
    st_spec = pl.BlockSpec((1, 2, hps, GLA_DV, GLA_DK), lambda b, h, j: (b, 0, h, 0, 0))
    return pl.pallas_call(
        _gla_body,
        out_shape=[jax.ShapeDtypeStruct((t, GLA_VW), BF16), jax.ShapeDtypeStruct((t, GLA_VW), BF16),
                   jax.ShapeDtypeStruct(s0.shape, F32)],
        grid=(batch, GLA_HEADS // hps, nc),
        in_specs=[pl.BlockSpec((c, kw), fwd), pl.BlockSpec((c, kw), fwd), pl.BlockSpec((c, vw), fwd),
                  pl.BlockSpec((c, LANES), fwd0),
                  pl.BlockSpec((c, kw), bwd), pl.BlockSpec((c, kw), bwd), pl.BlockSpec((c, vw), bwd),
                  pl.BlockSpec((c, LANES), bwd0),
                  pl.BlockSpec((2, LANES, kw), lambda b, h, j: (0, 0, h)),
                  pl.BlockSpec((2, 1, kw), lambda b, h, j: (0, 0, h)),
                  st_spec],
        out_specs=[pl.BlockSpec((c, vw), fwd), pl.BlockSpec((c, vw), bwd), st_spec],
        scratch_shapes=[pltpu.VMEM((2, hps, GLA_DV, GLA_DK), F32)],
        compiler_params=_cparams(("parallel", "parallel", "arbitrary")),
        name="gla_scan",
    )(q, k, v, lr, q, k, v, lr, up_pad, bias, s0)


FFT_P = 128


def _fft_tables(seq):
    q = seq // FFT_P
    n_hi = np.arange(q)
    ang1 = 2.0 * np.pi * ((n_hi[:, None] * n_hi[None, :]) % q) / q
    w1 = np.concatenate([np.cos(ang1), -np.sin(ang1)], axis=0)
    n_lo = np.arange(FFT_P)
    k_a = np.arange(FFT_P)
    k_b = np.arange(q)
    k_full = k_b[:, None, None] + q * k_a[None, :, None]
    ang2 = 2.0 * np.pi * ((k_full * n_lo[None, None, :]) % seq) / seq
    c2, s2 = np.cos(ang2), np.sin(ang2)
    w2 = np.concatenate([np.concatenate([c2, s2], axis=2), np.concatenate([-s2, c2], axis=2)], axis=1)
    ch = np.arange(FNET_GROUP_W)
    angc = 2.0 * np.pi * ((ch[:, None] * ch[None, :]) % FNET_GROUP_W) / FNET_GROUP_W
    scale = 1.0 / np.sqrt(float(seq) * FNET_GROUP_W)
    eye = np.eye(FNET_GROUPS)
    wc = np.concatenate([np.kron(eye, np.cos(angc)), np.kron(eye, np.sin(angc))], axis=0) * scale
    return tuple(jnp.asarray(w, F32).astype(BF16) for w in (w1, w2, wc))


def _fft1_body(z_ref, w_ref, o_ref):
    o_ref[0] = jnp.dot(w_ref[...], z_ref[0], preferred_element_type=F32).astype(o_ref.dtype)


FFT_KB = 4


def _fft2_body(a_ref, w_ref, wc_ref, o_ref):
    for i in range(a_ref.shape[2]):
        a = a_ref[0, :, i].reshape(2 * FFT_P, FNET_W)
        z = jnp.dot(w_ref[i], a, preferred_element_type=F32).astype(BF16)
        zz = jnp.concatenate([z[:FFT_P], z[FFT_P:]], axis=-1)
        o_ref[0, :, i * FNET_W:(i + 1) * FNET_W] = jnp.dot(zz, wc_ref[...],
                                                           preferred_element_type=F32).astype(o_ref.dtype)


def _fourier_mix(f, batch):
    t = f.shape[0]
    seq = t // batch
    q = seq // FFT_P
    w1, w2, wc = _fft_tables(seq)
    cols = FFT_P * FNET_W
    tn = min(8192, cols)
    z = f.reshape(batch, q, cols)
    a = pl.pallas_call(
        _fft1_body,
        out_shape=jax.ShapeDtypeStruct((batch, 2 * q, cols), BF16),
        grid=(batch, cols // tn),
        in_specs=[pl.BlockSpec((1, q, tn), lambda b, j: (b, 0, j)),
                  pl.BlockSpec((2 * q, q), lambda b, j: (0, 0))],
        out_specs=pl.BlockSpec((1, 2 * q, tn), lambda b, j: (b, 0, j)),
        compiler_params=_cparams(("parallel", "parallel")),
        name="fft_stage1",
    )(z, w1)
    a = a.reshape(batch, 2, q, FFT_P, FNET_W)
    kbs = min(FFT_KB, q)
    out = pl.pallas_call(
        _fft2_body,
        out_shape=jax.ShapeDtypeStruct((batch, FFT_P, q * FNET_W), BF16),
        grid=(batch, q // kbs),
        in_specs=[pl.BlockSpec((1, 2, kbs, FFT_P, FNET_W), lambda b, kb: (b, 0, kb, 0, 0)),
                  pl.BlockSpec((kbs, 2 * FFT_P, 2 * FFT_P), lambda b, kb: (kb, 0, 0)),
                  pl.BlockSpec((2 * FNET_W, FNET_W), lambda b, kb: (0, 0))],
        out_specs=pl.BlockSpec((1, FFT_P, kbs * FNET_W), lambda b, kb: (b, 0, kb)),
        compiler_params=_cparams(("parallel", "parallel")),
        name="fft_stage2",
    )(a, w2, wc)
    return out.reshape(batch, FFT_P, q, FNET_W).reshape(t, FNET_W)


def _rms(y, g):
    return y * lax.rsqrt(jnp.mean(y * y, axis=-1, keepdims=True) + EPS) * g


def _even_out_body(of_ref, ob_ref, r_ref, fo_ref, x_ref, mod_ref, gn_ref, w_ref, gpost_ref, gffn_ref, rw_ref, rb_ref,
                   o_ref, *route_refs):
    parts = []
    for h in range(GLA_HEADS):
        sl = slice(h * GLA_DV, (h + 1) * GLA_DV)
        o = of_ref[:, sl].astype(F32) + ob_ref[:, sl].astype(F32)
        r = r_ref[:, sl].astype(F32)
        parts.append((_rms(o, gn_ref[...]) * (r * jax.nn.sigmoid(r))).astype(BF16))
    y = jnp.dot(jnp.concatenate(parts, axis=-1), w_ref[:GLA_VW, :], preferred_element_type=F32)
    y = y + jnp.dot(fo_ref[...], w_ref[GLA_VW:, :], preferred_element_type=F32)
    x1 = x_ref[...] + mod_ref[0, 2:3, :] * _rms(y, gpost_ref[...])
    o_ref[...] = x1
    _route_tile(x1, mod_ref, gffn_ref, rw_ref, rb_ref, *route_refs)


def _even_out(o_f, o_b, r, fo, x2, mod, gnorm, w_out, g_post, route_params, batch):
    t, d = x2.shape
    tm = min(ROW_TILE, t)
    tiles_per_mod = (t // batch) // tm
    row = lambda w: pl.BlockSpec((tm, w), lambda i: (i, 0))
    full = lambda a: pl.BlockSpec(a.shape, lambda i: (0,) * a.ndim)
    gn = gnorm.reshape(1, GLA_DV)
    gp = g_post.reshape(1, d)
    r_args, r_in, r_shape, r_out, r_scratch = _route_io(*route_params, t, tm)
    outs = pl.pallas_call(
        _even_out_body,
        out_shape=[jax.ShapeDtypeStruct((t, d), F32)] + r_shape,
        grid=(t // tm,),
        in_specs=[row(GLA_VW), row(GLA_VW), row(GLA_VW), row(FNET_W), row(d),
                  pl.BlockSpec((1, 8, d), lambda i: (i // tiles_per_mod, 0, 0)),
                  full(gn), full(w_out), full(gp)] + r_in,
        out_specs=[row(d)] + r_out,
        scratch_shapes=r_scratch,
        compiler_params=_cparams(("arbitrary",)),
        name="even_out",
    )(o_f, o_b, r, fo, x2, mod, gn, w_out, gp, *r_args)
    return outs[0], tuple(outs[1:])


MOE_TILE = 512
MOE_BLOCK = 512
NEG_BIG = -1e30


def _route_tile(x, mod_ref, g_ref, rw_ref, rb_ref, h_ref, lpos_ref, gate_ref, cnt_ref, cbase_ref, base_ref):
    i = pl.program_id(0)

    @pl.when(i == 0)
    def _():
        base_ref[...] = jnp.zeros_like(base_ref)

    tm = x.shape[0]
    h = _modnorm(x, g_ref[...], mod_ref, 3, 4)
    h_hi = h.astype(BF16)
    h_ref[...] = h_hi
    h_lo = (h - h_hi.astype(F32)).astype(BF16)
    both = jnp.dot(h_hi, rw_ref[...], preferred_element_type=F32)
    logits = (both[:, :LANES] + both[:, LANES:]
              + jnp.dot(h_lo, rw_ref[:, :LANES], preferred_element_type=F32) + rb_ref[...])
    lane = lax.broadcasted_iota(jnp.int32, (tm, LANES), 1).astype(F32)
    work = logits
    vals, hots = [], []
    for _k in range(TOP_K):
        m = jnp.max(work, axis=-1, keepdims=True)
        idx = jnp.min(jnp.where(work == m, lane, float(LANES)), axis=-1, keepdims=True)
        hot = lane == idx
        vals.append(m)
        hots.append(hot)
        work = jnp.where(hot, -jnp.inf, work)
    es = [jnp.exp(v - vals[0]) for v in vals]
    inv = 1.0 / (es[0] + es[1] + es[2] + es[3])
    sel = sum(hh.astype(F32) for hh in hots)
    ri = lax.broadcasted_iota(jnp.int32, (tm, tm), 0)
    ci = lax.broadcasted_iota(jnp.int32, (tm, tm), 1)
    earlier = jnp.where(ci < ri, 1.0, 0.0).astype(BF16)
    cum = jnp.dot(earlier, sel.astype(BF16), preferred_element_type=F32)
    cnt = jnp.sum(sel, axis=0, keepdims=True)
    ei = lax.broadcasted_iota(jnp.int32, (LANES, LANES), 0)
    ej = lax.broadcasted_iota(jnp.int32, (LANES, LANES), 1)
    before = jnp.where(ei < ej, 1.0, 0.0)
    loff = jnp.dot(jnp.broadcast_to(cnt, (SUBLANES, LANES)), before, precision=lax.Precision.HIGHEST,
                   preferred_element_type=F32)[0:1]
    where_to = cum + loff
    for k in range(TOP_K):
        p = jnp.sum(jnp.where(hots[k], where_to, 0.0), axis=-1, keepdims=True)
        lpos_ref[:, k:k + 1] = p.astype(jnp.int32)
        gate_ref[:, k:k + 1] = es[k] * inv
    cnt_ref[0] = cnt.astype(jnp.int32)
    cbase_ref[0] = base_ref[...].astype(jnp.int32)
    base_ref[...] = base_ref[...] + cnt


def _route_io(g_pre, rw, rb, t, tm):
    d = D_MODEL
    nt = t // tm
    assert tm == min(MOE_TILE, t)
    rw_pad = jnp.zeros((d, LANES), F32).at[:, :N_EXPERTS].set(rw)
    rw_hi = rw_pad.astype(BF16)
    rw_pad = jnp.concatenate([rw_hi, (rw_pad - rw_hi.astype(F32)).astype(BF16)], axis=1)
    rb_pad = jnp.full((1, LANES), NEG_BIG, F32).at[0, :N_EXPERTS].set(rb)
    args = [g_pre.reshape(1, d), rw_pad, rb_pad]
    in_specs = [pl.BlockSpec(a.shape, lambda i: (0, 0)) for a in args]
    out_shape = [jax.ShapeDtypeStruct((t, d), BF16), jax.ShapeDtypeStruct((t, TOP_K), jnp.int32),
                 jax.ShapeDtypeStruct((t, TOP_K), F32), jax.ShapeDtypeStruct((nt, 1, LANES), jnp.int32),
                 jax.ShapeDtypeStruct((nt, 1, LANES), jnp.int32)]
    out_specs = [pl.BlockSpec((tm, d), lambda i: (i, 0)), pl.BlockSpec((tm, TOP_K), lambda i: (i, 0)),
                 pl.BlockSpec((tm, TOP_K), lambda i: (i, 0)), pl.BlockSpec((1, 1, LANES), lambda i: (i, 0, 0)),
                 pl.BlockSpec((1, 1, LANES), lambda i: (i, 0, 0))]
    return args, in_specs, out_shape, out_specs, [pltpu.VMEM((1, LANES), F32)]


def _moe_plan(cnt, cbase, n_slots):
    cnt = cnt[:, 0, :N_EXPERTS]
    cbase = cbase[:, 0, :N_EXPERTS]
    total = cbase[-1] + cnt[-1]
    padded = (total + MOE_BLOCK - 1) // MOE_BLOCK * MOE_BLOCK
    pad_end = jnp.cumsum(padded)
    start = pad_end - padded
    n_blocks = (n_slots + N_EXPERTS * (MOE_BLOCK - 1) + MOE_BLOCK - 1) // MOE_BLOCK
    n_used = pad_end[-1] // MOE_BLOCK
    blk = jnp.arange(n_blocks, dtype=jnp.int32)
    first_row = jnp.minimum(blk, n_used - 1) * MOE_BLOCK
    block_e = jnp.minimum(jnp.sum((pad_end[None, :] <= first_row[:, None]).astype(jnp.int32), axis=1), N_EXPERTS - 1)
    loff = jnp.cumsum(cnt, axis=1) - cnt
    gdst = start[None, :] + cbase
    flat = lambda a: a.reshape(-1).astype(jnp.int32)
    fill = (flat(padded - total), jnp.zeros((N_EXPERTS,), jnp.int32), flat(start + total))
    return (n_blocks, block_e, n_used.reshape(1).astype(jnp.int32), flat(cnt), flat(loff), flat(gdst)) + fill


ROW_SUB = D_MODEL // LANES


def _rows_to_tiles(ref, value):
    n = value.shape[0]
    for j in range(ROW_SUB):
        ref[pl.ds(j, n, stride=ROW_SUB), :] = value[:, j * LANES:(j + 1) * LANES]


def _tiles_to_rows(ref):
    n = ref.shape[0] // ROW_SUB
    return jnp.concatenate([ref[pl.ds(j, n, stride=ROW_SUB), :] for j in range(ROW_SUB)], axis=-1)


def _strip_copies(cnt_s, loff_s, gdst_s, tile, local_ref, global_ref, sem, to_global, max_rows):
    span = lambda start, size: pl.ds(pl.multiple_of(start * ROW_SUB, ROW_SUB), size * ROW_SUB)

    def per_expert(e, carry):
        n = cnt_s[tile * N_EXPERTS + e]
        src0 = loff_s[tile * N_EXPERTS + e]
        dst0 = gdst_s[tile * N_EXPERTS + e]
        off = jnp.int32(0)
        size = max_rows
        while size >= 1:
            take = n & size

            @pl.when(take != 0)
            def _(off=off, size=size):
                loc = local_ref.at[span(src0 + off, size)]
                glo = global_ref.at[span(dst0 + off, size)]
                if to_global:
                    pltpu.make_async_copy(loc, glo, sem).start()
                else:
                    pltpu.make_async_copy(glo, loc, sem).start()

            off = off + take
            size //= 2
        return carry

    lax.fori_loop(0, N_EXPERTS, per_expert, 0)


def _dispatch_body(cnt_s, loff_s, gdst_s, zcnt_s, zoff_s, zdst_s, nu_s, h_ref, lpos_ref, xb_ref, xs0_ref, xs1_ref, zero_ref,
                   sem0, sem1, zsem, *, n_blocks, n_fill, n_tiles):
    i = pl.program_id(0)
    tm = h_ref.shape[0]
    rows = TOP_K * tm
    lane = lax.broadcasted_iota(jnp.int32, (tm, rows), 1)
    onehot = jnp.zeros((tm, rows), F32)
    for k in range(TOP_K):
        onehot = jnp.where(lane == lpos_ref[:, k:k + 1], 1.0, onehot)
    xs = lax.dot_general(onehot.astype(BF16), h_ref[...], (((0,), (0,)), ((), ())), preferred_element_type=F32)
    slots = ((xs0_ref, sem0), (xs1_ref, sem1))

    def drain(slot):
        xs_ref, sem = slots[slot]
        pltpu.make_async_copy(xs_ref, xb_ref.at[pl.ds(0, rows * ROW_SUB)], sem).wait()

    for slot in range(2):
        @pl.when(i % 2 == slot)
        def _(slot=slot):
            @pl.when(i >= 2)
            def _():
                drain(slot)

            xs_ref, sem = slots[slot]
            _rows_to_tiles(xs_ref, xs)
            _strip_copies(cnt_s, loff_s, gdst_s, i, xs_ref, xb_ref, sem, True, tm)

    @pl.when(i == n_tiles - 1)
    def _():
        drain((n_tiles - 1) % 2)
        if n_tiles >= 2:
            drain(n_tiles % 2)
        zero_ref[...] = jnp.zeros_like(zero_ref)
        _strip_copies(zcnt_s, zoff_s, zdst_s, 0, zero_ref, xb_ref, zsem, True, MOE_BLOCK // 2)

        def per_block(b, carry):
            @pl.when(b >= nu_s[0])
            def _():
                pltpu.make_async_copy(zero_ref, xb_ref.at[pl.ds(b * (MOE_BLOCK * ROW_SUB), MOE_BLOCK * ROW_SUB)],
                                      zsem).start()
            return carry

        lax.fori_loop(0, n_blocks, per_block, 0)
        pltpu.make_async_copy(xb_ref.at[pl.ds(0, n_fill * ROW_SUB)], xb_ref.at[pl.ds(0, n_fill * ROW_SUB)], zsem).wait()


def _dispatch(h, lpos, plan):
    n_blocks, _, n_used, cnt_s, loff_s, gdst_s, zcnt_s, zoff_s, zdst_s = plan
    t, d = h.shape
    tm = min(MOE_TILE, t)
    rows = TOP_K * tm
    n_fill = n_blocks * MOE_BLOCK - TOP_K * t
    return pl.pallas_call(
        functools.partial(_dispatch_body, n_blocks=n_blocks, n_fill=n_fill, n_tiles=t // tm),
        out_shape=jax.ShapeDtypeStruct((n_blocks * MOE_BLOCK * ROW_SUB, LANES), F32),
        grid_spec=pltpu.PrefetchScalarGridSpec(
            num_scalar_prefetch=7,
            grid=(t // tm,),
            in_specs=[pl.BlockSpec((tm, d), lambda i, *_: (i, 0)),
                      pl.BlockSpec((tm, TOP_K), lambda i, *_: (i, 0))],
            out_specs=pl.BlockSpec(memory_space=pl.ANY),
            scratch_shapes=[pltpu.VMEM((rows * ROW_SUB, LANES), F32), pltpu.VMEM((rows * ROW_SUB, LANES), F32),
                            pltpu.VMEM((MOE_BLOCK * ROW_SUB, LANES), F32),
                            pltpu.SemaphoreType.DMA, pltpu.SemaphoreType.DMA, pltpu.SemaphoreType.DMA],
        ),
        compiler_params=_cparams(("arbitrary",)),
        name="moe_dispatch",
    )(cnt_s, loff_s, gdst_s, zcnt_s, zoff_s, zdst_s, n_used, h, lpos)


def _experts_body(be_s, nu_s, x_ref, wgu_ref, bgu_ref, wd_ref, bd_ref, o_ref, wgu_bf, wd_bf):
    i = pl.program_id(0)

    @pl.when(i < nu_s[0])
    def _():
        @pl.when((i == 0) | (be_s[i] != be_s[jnp.maximum(i - 1, 0)]))
        def _():
            wgu_bf[...] = wgu_ref[0].astype(BF16)
            wd_bf[...] = wd_ref[0].astype(BF16)

        x = _tiles_to_rows(x_ref).astype(BF16)
        gu = jnp.dot(x, wgu_bf[...], preferred_element_type=F32) + bgu_ref[0]
        g = jnp.minimum(gu[:, :D_FF], SWIGLU_LIMIT)
        u = jnp.clip(gu[:, D_FF:], -SWIGLU_LIMIT, SWIGLU_LIMIT)
        act = (u + 1.0) * (g * jax.nn.sigmoid(SWIGLU_ALPHA * g))
        y = jnp.dot(act.astype(BF16), wd_bf[...], preferred_element_type=F32) + bd_ref[0]
        _rows_to_tiles(o_ref, y)

    @pl.when(i >= nu_s[0])
    def _():
        o_ref[...] = jnp.zeros_like(o_ref)


def _experts(xb, plan, layer, w_gu, b_gu, w_down, b_down):
    n_blocks, block_e, n_used = plan[:3]
    d = D_MODEL
    base = layer * N_EXPERTS
    flat = lambda a: a.reshape((-1,) + a.shape[2:])
    slot_block = pl.BlockSpec((MOE_BLOCK * ROW_SUB, LANES), lambda i, be, nu: (i, 0))
    return pl.pallas_call(
        _experts_body,
        out_shape=jax.ShapeDtypeStruct(xb.shape, F32),
        grid_spec=pltpu.PrefetchScalarGridSpec(
            num_scalar_prefetch=2,
            grid=(n_blocks,),
            in_specs=[slot_block,
                      pl.BlockSpec((1, d, 2 * D_FF), lambda i, be, nu: (base + be[i], 0, 0)),
                      pl.BlockSpec((1, 1, 2 * D_FF), lambda i, be, nu: (base + be[i], 0, 0)),
                      pl.BlockSpec((1, D_FF, d), lambda i, be, nu: (base + be[i], 0, 0)),
                      pl.BlockSpec((1, 1, d), lambda i, be, nu: (base + be[i], 0, 0))],
            out_specs=slot_block,
            scratch_shapes=[pltpu.VMEM((d, 2 * D_FF), BF16), pltpu.VMEM((D_FF, d), BF16)],
        ),
        compiler_params=_cparams(("arbitrary",)),
        name="moe_experts",
    )(block_e, n_used, xb, flat(w_gu), flat(b_gu)[:, None, :], flat(w_down), flat(b_down)[:, None, :])


def _combine_body(cnt_s, loff_s, gdst_s, yb_ref, lpos_ref, gate_ref, x_ref, mod_ref, g_ref, o_ref, ys0_ref, ys1_ref,
                  ys2_ref, sem0, sem1, *, n_tiles):
    i = pl.program_id(0)
    tm = x_ref.shape[0]
    rows = TOP_K * tm
    slots = ((ys0_ref, sem0), (ys1_ref, sem1))

    def fetch(tile, slot):
        ys_ref, sem = slots[slot]
        _strip_copies(cnt_s, loff_s, gdst_s, tile, ys_ref, yb_ref, sem, False, tm)

    @pl.when(i == 0)
    def _():
        fetch(0, 0)

    for slot in range(2):
        @pl.when(i % 2 == slot)
        def _(slot=slot):
            @pl.when(i + 1 < n_tiles)
            def _():
                fetch(i + 1, 1 - slot)

            ys_ref, sem = slots[slot]
            pltpu.make_async_copy(yb_ref.at[pl.ds(0, rows * ROW_SUB)], ys_ref, sem).wait()
            ys2_ref[...] = _tiles_to_rows(ys_ref).astype(BF16)

    lane = lax.broadcasted_iota(jnp.int32, (tm, rows), 1)
    weights = jnp.zeros((tm, rows), F32)
    for k in range(TOP_K):
        weights = jnp.where(lane == lpos_ref[:, k:k + 1], gate_ref[:, k:k + 1], weights)
    f = jnp.dot(weights.astype(BF16), ys2_ref[...], preferred_element_type=F32)
    o_ref[...] = x_ref[...] + mod_ref[0, 5:6, :] * _rms(f, g_ref[...])


def _combine(yb, lpos, gate, x2, mod, g_post, plan, batch):
    cnt_s, loff_s, gdst_s = plan[3:6]
    t, d = x2.shape
    tm = min(MOE_TILE, t)
    rows = TOP_K * tm
    tiles_per_mod = (t // batch) // tm
    return pl.pallas_call(
        functools.partial(_combine_body, n_tiles=t // tm),
        out_shape=jax.ShapeDtypeStruct((t, d), F32),
        grid_spec=pltpu.PrefetchScalarGridSpec(
            num_scalar_prefetch=3,
            grid=(t // tm,),
            in_specs=[pl.BlockSpec(memory_space=pl.ANY),
                      pl.BlockSpec((tm, TOP_K), lambda i, *_: (i, 0)),
                      pl.BlockSpec((tm, TOP_K), lambda i, *_: (i, 0)),
                      pl.BlockSpec((tm, d), lambda i, *_: (i, 0)),
                      pl.BlockSpec((1, 8, d), lambda i, *_: (i // tiles_per_mod, 0, 0)),
                      pl.BlockSpec((1, d), lambda i, *_: (0, 0))],
            out_specs=pl.BlockSpec((tm, d), lambda i, *_: (i, 0)),
            scratch_shapes=[pltpu.VMEM((rows * ROW_SUB, LANES), F32), pltpu.VMEM((rows * ROW_SUB, LANES), F32),
                            pltpu.VMEM((rows, d), BF16),
                            pltpu.SemaphoreType.DMA, pltpu.SemaphoreType.DMA],
        ),
        compiler_params=_cparams(("arbitrary",)),
        name="moe_combine",
    )(cnt_s, loff_s, gdst_s, yb, lpos, gate, x2, mod, g_post.reshape(1, d))


def _moe_ffn(x2, routing, mod, g_post, layer, w_gu, b_gu, w_down, b_down, batch):
    h, lpos, gate, cnt, cbase = routing
    plan = _moe_plan(cnt, cbase, x2.shape[0] * TOP_K)
    xb = _dispatch(h, lpos, plan)
    yb = _experts(xb, plan, layer, w_gu, b_gu, w_down, b_down)
    return _combine(yb, lpos, gate, x2, mod, g_post, plan, batch)


CONF_HALO = 16


def _odd_body(a_ref, ap_ref, an_ref, gb_ref, cu_ref, cup_ref, cun_ref, x_ref, mod_ref, dw_ref, dwb_ref, lng_ref,
              lnb_ref, sc_ref, w_ref, gpost_ref, gffn_ref, rw_ref, rb_ref, o_ref, h_ref, lpos_ref, gate_ref, cnt_ref,
              cbase_ref, shift_ref, base_ref, *, tiles_per_seq):
    i = pl.program_id(0)
    tm = x_ref.shape[0]
    first = (i % tiles_per_seq) == 0
    last = (i % tiles_per_seq) == tiles_per_seq - 1

    def glu(ref):
        v = ref[...].astype(F32)
        return v[:, :CONF_W] * jax.nn.sigmoid(v[:, CONF_W:])

    pad = CONF_KERNEL // 2
    ext = jnp.concatenate([jnp.where(first, 0.0, glu(ap_ref)), glu(a_ref), jnp.where(last, 0.0, glu(an_ref))], axis=0)
    acc = jnp.zeros((tm, CONF_W), F32) + dwb_ref[...]
    span = shift_ref.shape[1]
    for r in range(SUBLANES):
        shift_ref[r] = ext[r:r + span, :]
    for k in range(CONF_KERNEL):
        lo = CONF_HALO - pad + k
        base = (lo // SUBLANES) * SUBLANES
        acc = acc + dw_ref[k:k + 1, :] * shift_ref[lo % SUBLANES, base:base + tm, :]
    mu = jnp.mean(acc, axis=-1, keepdims=True)
    xc = acc - mu
    hn = xc * lax.rsqrt(jnp.mean(xc * xc, axis=-1, keepdims=True) + EPS) * lng_ref[...] + lnb_ref[...]
    hc = (hn * jax.nn.sigmoid(hn)).astype(BF16)

    def gated(ref):
        v = ref[...].astype(F32)
        return v[:, :SCONV_W] * v[:, SCONV_W:]

    zc = gated(cu_ref)
    zext = jnp.concatenate([jnp.where(first, 0.0, gated(cup_ref)), zc, jnp.where(last, 0.0, gated(cun_ref))], axis=0)
    z = sc_ref[0:1, :] * zext[0:tm] + sc_ref[1:2, :] * zc + sc_ref[2:3, :] * zext[2 * GRID_W:2 * GRID_W + tm]
    z = (gb_ref[...].astype(F32) * z).astype(BF16)
    y = jnp.dot(hc, w_ref[:CONF_W, :], preferred_element_type=F32) + jnp.dot(z, w_ref[CONF_W:, :],
                                                                            preferred_element_type=F32)
    x1 = x_ref[...] + mod_ref[0, 2:3, :] * _rms(y, gpost_ref[...])
    o_ref[...] = x1
    _route_tile(x1, mod_ref, gffn_ref, rw_ref, rb_ref, h_ref, lpos_ref, gate_ref, cnt_ref, cbase_ref, base_ref)


def _odd_mix(a, gb, cu, x2, mod, conf_dw, conf_dw_b, ln_g, ln_b, sconv, w_out, g_post, route_params, batch):
    t, d = x2.shape
    seq = t // batch
    tm = min(ROW_TILE, seq)
    tiles_per_seq = seq // tm
    nt = t // tm
    hb_c, hb_s = tm // CONF_HALO, tm // GRID_W
    row = lambda w: pl.BlockSpec((tm, w), lambda i: (i, 0))
    prev = lambda rows, per, w: pl.BlockSpec((rows, w), lambda i: (jnp.maximum(i * per - 1, 0), 0))
    nxt = lambda rows, per, w: pl.BlockSpec((rows, w), lambda i: (jnp.minimum((i + 1) * per, nt * per - 1), 0))
    full = lambda arr: pl.BlockSpec(arr.shape, lambda i: (0,) * arr.ndim)
    smalls = [conf_dw, conf_dw_b.reshape(1, -1), ln_g.reshape(1, -1), ln_b.reshape(1, -1), sconv, w_out,
              g_post.reshape(1, d)]
    r_args, r_in, r_shape, r_out, r_scratch = _route_io(*route_params, t, tm)
    outs = pl.pallas_call(
        functools.partial(_odd_body, tiles_per_seq=tiles_per_seq),
        out_shape=[jax.ShapeDtypeStruct((t, d), F32)] + r_shape,
        grid=(nt,),
        in_specs=[row(2 * CONF_W), prev(CONF_HALO, hb_c, 2 * CONF_W), nxt(CONF_HALO, hb_c, 2 * CONF_W),
                  row(SCONV_W),
                  row(2 * SCONV_W), prev(GRID_W, hb_s, 2 * SCONV_W), nxt(GRID_W, hb_s, 2 * SCONV_W),
                  row(d), pl.BlockSpec((1, 8, d), lambda i: (i // tiles_per_seq, 0, 0))]
                 + [full(s) for s in smalls] + r_in,
        out_specs=[row(d)] + r_out,
        scratch_shapes=[pltpu.VMEM((SUBLANES, tm + CONF_HALO + SUBLANES, CONF_W), F32)] + r_scratch,
        compiler_params=_cparams(("arbitrary",)),
        name="odd_mix",
    )(a, a, a, gb, cu, cu, cu, x2, mod, *smalls, *r_args)
    return outs[0], tuple(outs[1:])


COND_ROWS = 16


def _mod_rows(mods_layer):
    m = mods_layer.reshape(COND_ROWS, 6, D_MODEL)
    return jnp.concatenate([m, jnp.zeros((COND_ROWS, 2, D_MODEL), F32)], axis=1)


def _even_weights(w_in, gk_up, gk_b):
    cuts = np.cumsum((GLA_KW, GLA_KW, GLA_VW, GLA_VW, GLA_LOWRANK, GLA_LOWRANK, FNET_W))[:-1]
    wq, wk, wv, wr, wlf, wlb, wf = jnp.split(w_in, [int(v) for v in cuts], axis=-1)
    w_main = jnp.concatenate([wq, wk, wv, wr, wf], axis=-1).astype(BF16)
    pad = jnp.zeros((w_in.shape[0], LANES - 2 * GLA_LOWRANK), w_in.dtype)
    w_tail = jnp.concatenate([wlf, wlb, pad], axis=-1).astype(BF16)
    up_pad = jnp.zeros((2, LANES, GLA_KW), F32)
    up_pad = up_pad.at[0, :GLA_LOWRANK].set(gk_up[0]).at[1, GLA_LOWRANK:2 * GLA_LOWRANK].set(gk_up[1])
    return w_main, w_tail, up_pad, gk_b.reshape(2, 1, GLA_KW)


EVEN_WIDTHS = (GLA_KW, GLA_KW, GLA_VW, GLA_VW, FNET_W)


def _even_gla(x2, ctx2, mod, g_pre, w_main, w_tail, up_pad, bias, batch):
    nb = batch
    cq, ck, cv, _, _, clr = _norm_proj(ctx2, mod[nb:nb + 1], ctx2.shape[0], g_pre, w_main, EVEN_WIDTHS, w_tail)
    zero = jnp.zeros((nb, 2, GLA_HEADS, GLA_DV, GLA_DK), F32)
    _, _, s_ctx = _gla(cq, ck, cv, clr, up_pad, bias, zero, nb)
    q, k, v, r, f, lr = _norm_proj(x2, mod[:nb], x2.shape[0] // nb, g_pre, w_main, EVEN_WIDTHS, w_tail)
    o_f, o_b, _ = _gla(q, k, v, lr, up_pad, bias, s_ctx, nb)
    return r, f, o_f, o_b


def kernel(x, c, ctx, c_ctx, mod_w, mod_b, norm_mix_pre, norm_mix_post, norm_ffn_pre, norm_ffn_post, ev_w_in, ev_gk_up, ev_gk_b, ev_gnorm, ev_w_out, od_w_in, od_conf_dw, od_conf_dw_b, od_conf_ln_g, od_conf_ln_b, od_sconv, od_w_out, router_w, router_b, exp_w_gu, exp_b_gu, exp_w_down, exp_b_down):
    nb, seq, d = x.shape
    cond = jnp.concatenate([c, c_ctx[None], jnp.zeros((COND_ROWS - nb - 1, d), F32)], axis=0)
    mods = _adaln(cond, mod_w, mod_b)
    x2 = x.reshape(nb * seq, d)
    ctx2 = ctx.reshape(nb * ctx.shape[1], d)
    depth = mod_w.shape[0]
    assert depth == 2, "layer pattern implemented for one even layer followed by one odd layer"
    mod = _mod_rows(mods[0])
    wm, wt, up, bias = _even_weights(ev_w_in[0], ev_gk_up[0], ev_gk_b[0])
    r, f, o_f, o_b = _even_gla(x2, ctx2, mod, norm_mix_pre[0], wm, wt, up, bias, nb)
    fo = _fourier_mix(f, nb)
    x2, routing = _even_out(o_f, o_b, r, fo, x2, mod, ev_gnorm[0], ev_w_out[0].astype(BF16), norm_mix_post[0],
                            (norm_ffn_pre[0], router_w[0], router_b[0]), nb)
    x2 = _moe_ffn(x2, routing, mod, norm_ffn_post[0], 0, exp_w_gu, exp_b_gu, exp_w_down, exp_b_down, nb)
    mod = _mod_rows(mods[1])
    a, gb, cu = _norm_proj(x2, mod[:nb], seq, norm_mix_pre[1], od_w_in[0].astype(BF16),
                           (2 * CONF_W, SCONV_W, 2 * SCONV_W))
    x2, routing = _odd_mix(a, gb, cu, x2, mod, od_conf_dw[0], od_conf_dw_b[0], od_conf_ln_g[0], od_conf_ln_b[0],
                           od_sconv[0], od_w_out[0].astype(BF16), norm_mix_post[1],
                           (norm_ffn_pre[1], router_w[1], router_b[1]), nb)
    x2 = _moe_ffn(x2, routing, mod, norm_ffn_post[1], 1, exp_w_gu, exp_b_gu, exp_w_down, exp_b_down, nb)
    return x2.reshape(nb, seq, d)
```
